```python
import math
import jax, jax.numpy as jnp
from jax import lax
import numpy as np

D_MODEL = 1024
BATCH = 4
SEQ = 4096
DEPTH = 4
DEC_BATCH = 32
DEC_SEQ = 4
PAST_LEN = 8192
PAGE_SIZE = 128

CONV_W = D_MODEL // 4
CONV_K = 31
DIFF_H = 4
DIFF_DQK = 64
DIFF_DV = 2 * DIFF_DQK
DIFF_W = DIFF_H * DIFF_DV
RET_H = 4
RET_DK = 64
RET_DV = 64
RET_W = RET_H * RET_DV
MIX_W = CONV_W + DIFF_W + RET_W
D_FF = 4 * D_MODEL
ROPE_THETA = 500000.0
ROPE_ROT = DIFF_DQK // 4
RET_THETA = 10000.0
Q_BLOCK = 128
RET_CHUNK = 128
EPS = 1e-5
PROJ_SPLITS = (2 * CONV_W, 2 * DIFF_H * DIFF_DQK, 2 * DIFF_H * DIFF_DQK, DIFF_W,
               RET_H * RET_DK, RET_H * RET_DK, RET_W, RET_W)
D_IN = sum(PROJ_SPLITS)

kernel_name = "hybrid_conv_diffattn_retention_decoder_step"


def rmsnorm(x, g):
    xf = x.astype(jnp.float32)
    y = xf * lax.rsqrt(jnp.mean(xf * xf, axis=-1, keepdims=True) + EPS)
    return (y * g.astype(jnp.float32)).astype(x.dtype)


def layernorm(x, g, b):
    xf = x.astype(jnp.float32)
    mu = jnp.mean(xf, axis=-1, keepdims=True)
    var = jnp.mean(jnp.square(xf - mu), axis=-1, keepdims=True)
    return (xf - mu) * lax.rsqrt(var + EPS) * g.astype(jnp.float32) + b.astype(jnp.float32)


def rope(x, pos, n_rot, theta):
    half = n_rot // 2
    inv = 1.0 / (theta ** (jnp.arange(half, dtype=jnp.float32) * 2.0 / n_rot))
    ang = pos.astype(jnp.float32)[:, None] * inv[None, :]
    cos = jnp.cos(ang)[None, :, None, :]
    sin = jnp.sin(ang)[None, :, None, :]
    xr = x[..., :n_rot].astype(jnp.float32)
    x1, x2 = xr[..., :half], xr[..., half:]
    rot = jnp.concatenate([x1 * cos - x2 * sin, x2 * cos + x1 * sin], axis=-1).astype(x.dtype)
    return jnp.concatenate([rot, x[..., n_rot:]], axis=-1)


def depthwise_causal(u_ext, w, b):
    out = lax.conv_general_dilated(u_ext, w[:, None, :], window_strides=(1,), padding='VALID',
                                   dimension_numbers=('NWC', 'WIO', 'NWC'),
                                   feature_group_count=CONV_W)
    return out + b


def diff_combine(p, lam):
    p = p.reshape(p.shape[0], DIFF_H, 2, p.shape[2], p.shape[3])
    return p[:, :, 0] - lam * p[:, :, 1]


def diff_attn_prompt(q, k, v, lam):
    b, s = q.shape[0], q.shape[1]
    nb = s // Q_BLOCK
    qb = q.reshape(b, nb, Q_BLOCK, 2 * DIFF_H, DIFF_DQK).swapaxes(0, 1)
    kpos = jnp.arange(s)
    scale = DIFF_DQK ** -0.5

    def block(args):
        qi, bi = args
        qpos = bi * Q_BLOCK + jnp.arange(Q_BLOCK)
        sc = jnp.einsum('bqhd,bkhd->bhqk', qi, k).astype(jnp.float32) * scale
        sc = jnp.where(kpos[None, :] <= qpos[:, None], sc, -jnp.inf)
        a = diff_combine(jax.nn.softmax(sc, axis=-1), lam).astype(v.dtype)
        return jnp.einsum('bhqk,bkhe->bqhe', a, v)

    out = lax.map(block, (qb, jnp.arange(nb)))
    return out.swapaxes(0, 1).reshape(b, s, DIFF_H, DIFF_DV)


def diff_attn_sample(q, k_new, v_new, k_past, v_past, lam):
    scale = DIFF_DQK ** -0.5
    t = q.shape[1]
    s_past = jnp.einsum('bqhd,bkhd->bhqk', q, k_past).astype(jnp.float32) * scale
    s_new = jnp.einsum('bqhd,bkhd->bhqk', q, k_new).astype(jnp.float32) * scale
    causal = jnp.arange(t)[None, :] <= jnp.arange(t)[:, None]
    s_new = jnp.where(causal, s_new, -jnp.inf)
    p = jax.nn.softmax(jnp.concatenate([s_past, s_new], axis=-1), axis=-1)
    a = diff_combine(p, lam).astype(v_new.dtype)
    n_past = k_past.shape[1]
    return (jnp.einsum('bhqk,bkhe->bqhe', a[..., :n_past], v_past)
            + jnp.einsum('bhqk,bkhe->bqhe', a[..., n_past:], v_new))


def retention_chunk(S, q, k, v, log_gamma):
    q, k, v = q.astype(jnp.float32), k.astype(jnp.float32), v.astype(jnp.float32)
    L = q.shape[1]
    idx = jnp.arange(L, dtype=jnp.float32)
    diff = idx[:, None] - idx[None, :]
    decay = jnp.where(diff >= 0, jnp.exp(jnp.maximum(diff, 0.0)[None] * log_gamma[:, None, None]), 0.0)
    o = jnp.einsum('bihd,bjhd->bhij', q, k) * decay[None]
    o = jnp.einsum('bhij,bjhe->bihe', o, v)
    cross = jnp.exp((idx[:, None] + 1.0) * log_gamma[None, :])
    o = o + jnp.einsum('bihd,bhde->bihe', q, S) * cross[None, :, :, None]
    tail = jnp.exp((L - 1.0 - idx)[:, None] * log_gamma[None, :])
    S = (jnp.exp(L * log_gamma)[None, :, None, None] * S
         + jnp.einsum('bjhd,bjhe,jh->bhde', k, v, tail))
    return S, o


def retention_prompt(q, k, v, log_gamma):
    b, s = q.shape[0], q.shape[1]
    nc = s // RET_CHUNK

    def to_chunks(t):
        return t.reshape(b, nc, RET_CHUNK, t.shape[2], t.shape[3]).swapaxes(0, 1)

    S0 = jnp.zeros((b, RET_H, RET_DK, RET_DV), jnp.float32)

    def step(S, qkv):
        return retention_chunk(S, qkv[0], qkv[1], qkv[2], log_gamma)

    S_fin, o = lax.scan(step, S0, (to_chunks(q), to_chunks(k), to_chunks(v)))
    return o.swapaxes(0, 1).reshape(b, s, RET_H, RET_DV), S_fin


def trunk_layer(x, pos, W, l, past):
    b, s, _ = x.shape
    h = rmsnorm(x, W['norm_mix'][l])
    proj = h @ W['w_in'][l]
    c_ag, q_d, k_d, v_d, q_r, k_r, v_r, g_r = jnp.split(
        proj, np.cumsum(PROJ_SPLITS)[:-1].tolist(), axis=-1)
    u = c_ag[..., :CONV_W] * jax.nn.sigmoid(c_ag[..., CONV_W:])
    q_d = rope(q_d.reshape(b, s, 2 * DIFF_H, DIFF_DQK), pos, ROPE_ROT, ROPE_THETA)
    k_d = rope(k_d.reshape(b, s, 2 * DIFF_H, DIFF_DQK), pos, ROPE_ROT, ROPE_THETA)
    v_d = v_d.reshape(b, s, DIFF_H, DIFF_DV)
    lam_init = 0.8 - 0.6 * math.exp(-0.3 * l)
    lam = (jnp.exp(jnp.sum(W['lam_q1'][l] * W['lam_k1'][l]).astype(jnp.float32))
           - jnp.exp(jnp.sum(W['lam_q2'][l] * W['lam_k2'][l]).astype(jnp.float32)) + lam_init)
    q_r = rope(q_r.reshape(b, s, RET_H, RET_DK), pos, RET_DK, RET_THETA)
    k_r = rope(k_r.reshape(b, s, RET_H, RET_DK), pos, RET_DK, RET_THETA) * (RET_DK ** -0.5)
    v_r = v_r.reshape(b, s, RET_H, RET_DV)
    log_gamma = jnp.log(1.0 - jnp.exp2(-5.0 - jnp.arange(RET_H, dtype=jnp.float32)))

    if past is None:
        u_ext = jnp.pad(u, ((0, 0), (CONV_K - 1, 0), (0, 0)))
        o_d = diff_attn_prompt(q_d, k_d, v_d, lam)
        o_r, S_new = retention_prompt(q_r, k_r, v_r, log_gamma)
    else:
        conv_buf, k_past, v_past, S_prev = past
        u_ext = jnp.concatenate([conv_buf, u], axis=1)
        o_d = diff_attn_sample(q_d, k_d, v_d, k_past, v_past, lam)
        S_new, o_r = retention_chunk(S_prev.astype(jnp.float32), q_r, k_r, v_r, log_gamma)
    conv_new = u_ext[:, -(CONV_K - 1):]

    c = jax.nn.silu(layernorm(depthwise_causal(u_ext, W['conv_w'][l], W['conv_b'][l]),
                              W['conv_ln_g'][l], W['conv_ln_b'][l]))
    o_d = (rmsnorm(o_d, W['diff_subln'][l]) * (1.0 - lam_init)).reshape(b, s, DIFF_W)
    of = o_r.astype(jnp.float32)
    mu = jnp.mean(of, axis=-1, keepdims=True)
    var = jnp.mean(jnp.square(of - mu), axis=-1, keepdims=True)
    o_r = (((of - mu) * lax.rsqrt(var + EPS)).reshape(b, s, RET_W) * W['ret_gn'][l].astype(jnp.float32)
           * jax.nn.silu(g_r.astype(jnp.float32)))
    mix = jnp.concatenate([c.astype(x.dtype), o_d.astype(x.dtype), o_r.astype(x.dtype)], axis=-1)
    x = x + mix @ W['w_out'][l]
    h2 = rmsnorm(x, W['norm_mlp'][l])
    x = x + jnp.square(jax.nn.relu(h2 @ W['w_up'][l])) @ W['w_down'][l]
    return x, (k_d, v_d, conv_new, S_new.astype(x.dtype))


def setup_inputs(seed: int = 0) -> dict:
    key = jax.random.key(seed)
    ks = jax.random.split(key, 26)
    n_pages = PAST_LEN // PAGE_SIZE
    n_phys = (5 * DEC_BATCH * n_pages + 3) // 4

    def nrm(k, shape, scale):
        return jax.random.normal(k, shape, jnp.float32) * scale

    page_table = jax.random.permutation(ks[0], n_phys)[:DEC_BATCH * n_pages].reshape(
        DEC_BATCH, n_pages).astype(jnp.int32)
    return {
        "x_prompt": nrm(ks[1], (BATCH, SEQ, D_MODEL), 1.0),
        "x_sample": nrm(ks[2], (DEC_BATCH, DEC_SEQ, D_MODEL), 1.0),
        "cache_k": nrm(ks[3], (DEPTH, n_phys, PAGE_SIZE, 2 * DIFF_H, DIFF_DQK), 1.0),
        "cache_v": nrm(ks[4], (DEPTH, n_phys, PAGE_SIZE, DIFF_H, DIFF_DV), 1.0),
        "state_conv": nrm(ks[5], (DEPTH, DEC_BATCH, CONV_K - 1, CONV_W), 0.5),
        "state_ret": nrm(ks[6], (DEPTH, DEC_BATCH, RET_H, RET_DK, RET_DV), 1.0),
        "page_table": page_table,
        "norm_mix": 1.0 + nrm(ks[7], (DEPTH, D_MODEL), 0.02),
        "w_in": nrm(ks[8], (DEPTH, D_MODEL, D_IN), D_MODEL ** -0.5),
        "conv_w": nrm(ks[9], (DEPTH, CONV_K, CONV_W), CONV_K ** -0.5),
        "conv_b": nrm(ks[10], (DEPTH, CONV_W), 0.01),
        "conv_ln_g": 1.0 + nrm(ks[11], (DEPTH, CONV_W), 0.02),
        "conv_ln_b": nrm(ks[12], (DEPTH, CONV_W), 0.01),
        "lam_q1": nrm(ks[13], (DEPTH, DIFF_DQK), 0.1),
        "lam_k1": nrm(ks[14], (DEPTH, DIFF_DQK), 0.1),
        "lam_q2": nrm(ks[15], (DEPTH, DIFF_DQK), 0.1),
        "lam_k2": nrm(ks[16], (DEPTH, DIFF_DQK), 0.1),
        "diff_subln": 1.0 + nrm(ks[17], (DEPTH, DIFF_DV), 0.02),
        "ret_gn": 1.0 + nrm(ks[18], (DEPTH, RET_W), 0.02),
        "w_out": nrm(ks[19], (DEPTH, MIX_W, D_MODEL), MIX_W ** -0.5),
        "norm_mlp": 1.0 + nrm(ks[20], (DEPTH, D_MODEL), 0.02),
        "w_up": nrm(ks[21], (DEPTH, D_MODEL, D_FF), D_MODEL ** -0.5),
        "w_down": nrm(ks[22], (DEPTH, D_FF, D_MODEL), 0.5 * D_FF ** -0.5),
        "norm_final": 1.0 + nrm(ks[23], (D_MODEL,), 0.02),
    }


def reference(x_prompt, x_sample, cache_k, cache_v, state_conv, state_ret, page_table,
              norm_mix, w_in, conv_w, conv_b, conv_ln_g, conv_ln_b,
              lam_q1, lam_k1, lam_q2, lam_k2, diff_subln, ret_gn, w_out,
              norm_mlp, w_up, w_down, norm_final):
    W = dict(norm_mix=norm_mix, w_in=w_in, conv_w=conv_w, conv_b=conv_b,
             conv_ln_g=conv_ln_g, conv_ln_b=conv_ln_b, lam_q1=lam_q1, lam_k1=lam_k1,
             lam_q2=lam_q2, lam_k2=lam_k2, diff_subln=diff_subln, ret_gn=ret_gn,
             w_out=w_out, norm_mlp=norm_mlp, w_up=w_up, w_down=w_down)
    db = x_sample.shape[0]
    past_len = page_table.shape[1] * cache_k.shape[2]
    pos_p = jnp.arange(x_prompt.shape[1])
    pos_s = past_len + jnp.arange(x_sample.shape[1])
    xp, xs = x_prompt, x_sample
    kp, vp, cp, rp, ks_, vs_, cs_, rs_ = [], [], [], [], [], [], [], []
    for l in range(DEPTH):
        xp, (k_n, v_n, c_n, r_n) = trunk_layer(xp, pos_p, W, l, None)
        kp.append(k_n); vp.append(v_n); cp.append(c_n); rp.append(r_n)
        k_past = cache_k[l, page_table].reshape(db, past_len, 2 * DIFF_H, DIFF_DQK)
        v_past = cache_v[l, page_table].reshape(db, past_len, DIFF_H, DIFF_DV)
        xs, (k_n, v_n, c_n, r_n) = trunk_layer(
            xs, pos_s, W, l, (state_conv[l], k_past, v_past, state_ret[l]))
        ks_.append(k_n); vs_.append(v_n); cs_.append(c_n); rs_.append(r_n)
    y_prompt = rmsnorm(xp, norm_final)
    y_sample = rmsnorm(xs, norm_final)
    return (y_prompt, y_sample,
            jnp.stack(kp), jnp.stack(vp), jnp.stack(cp), jnp.stack(rp),
            jnp.stack(ks_), jnp.stack(vs_), jnp.stack(cs_), jnp.stack(rs_))
```

```python
import functools
import math

import jax
import jax.numpy as jnp
from jax import lax
from jax.experimental import pallas as pl
from jax.experimental.pallas import tpu as pltpu

F32 = jnp.float32
BF16 = jnp.bfloat16

CONV_W = 256
CONV_K = 31
DIFF_H = 4
DIFF_DQK = 64
DIFF_DV = 128
DIFF_W = DIFF_H * DIFF_DV
RET_H = 4
RET_DK = 64
RET_DV = 64
RET_W = RET_H * RET_DV
ROPE_THETA = 500000.0
ROPE_ROT = DIFF_DQK // 4
RET_THETA = 10000.0
EPS = 1e-5
LANES = 128
CONV_HALO = 32
VMEM_LIMIT = 48 * 1024 * 1024

_OFF_C, _OFF_QD, _OFF_KD, _OFF_VD, _OFF_QR, _OFF_KR, _OFF_VR, _OFF_GR, _D_IN = (
    0, 512, 1024, 1536, 2048, 2304, 2560, 2816, 3072)

_NT = (((1,), (1,)), ((), ()))


def _params(*sem):
    return pltpu.CompilerParams(dimension_semantics=sem, vmem_limit_bytes=VMEM_LIMIT)


def _rms(x, g):
    return x * lax.rsqrt(jnp.mean(x * x, axis=-1, keepdims=True) + EPS) * g


def _silu(x):
    return x / (1.0 + jnp.exp(-x))


def _lam(lam_ref, lam_init):
    s1 = jnp.sum(lam_ref[0:1, :] * lam_ref[1:2, :], axis=-1, keepdims=True)
    s2 = jnp.sum(lam_ref[2:3, :] * lam_ref[3:4, :], axis=-1, keepdims=True)
    return jnp.exp(s1) - jnp.exp(s2) + lam_init


def _rope_table(pos):
    posf = pos.astype(F32)[:, None]
    n = pos.shape[0]

    def one(n_rot, theta, hd):
        half = n_rot // 2
        inv = 1.0 / (theta ** (jnp.arange(half, dtype=F32) * 2.0 / n_rot))
        ang = posf * inv[None, :]
        cos, sin = jnp.cos(ang), jnp.sin(ang)
        c = jnp.concatenate([cos, cos, jnp.ones((n, hd - n_rot), F32)], axis=-1)
        s = jnp.concatenate([-sin, sin, jnp.zeros((n, hd - n_rot), F32)], axis=-1)
        return jnp.tile(c, (1, LANES // hd)), jnp.tile(s, (1, LANES // hd))

    cd, sd = one(ROPE_ROT, ROPE_THETA, DIFF_DQK)
    cr, sr = one(RET_DK, RET_THETA, RET_DK)
    return jnp.concatenate([cd, sd, cr, sr], axis=-1)


def _log_gamma():
    return jnp.log(1.0 - jnp.exp2(-5.0 - jnp.arange(RET_H, dtype=F32)))


def _ret_tables(L):
    lg = _log_gamma()
    idx = jnp.arange(L, dtype=F32)
    diff = idx[:, None] - idx[None, :]
    decay = jnp.where(diff >= 0, jnp.exp(jnp.maximum(diff, 0.0)[None] * lg[:, None, None]), 0.0)
    cross = jnp.exp((idx[:, None] + 1.0) * lg[None, :])
    tail = jnp.exp((L - 1.0 - idx)[:, None] * lg[None, :])
    gl = jnp.exp(L * lg)
    return decay, cross, tail, gl


def _inproj_kernel(x_ref, g_ref, w_ref, t_ref, u_ref, qkv_ref, k_ref, v_ref, ret_ref, gr_ref):
    h = _rms(x_ref[...], g_ref[...]).astype(BF16)

    def proj(a, b):
        return jnp.dot(h, w_ref[:, a:b], preferred_element_type=F32)

    lane = lax.broadcasted_iota(jnp.int32, (1, LANES), 1) % 64

    def rope(xg, c, s, half):
        partner = jnp.where(lane < half, pltpu.roll(xg, LANES - half, 1), pltpu.roll(xg, half, 1))
        return xg * c + partner * s

    cd, sd = t_ref[:, 0:128], t_ref[:, 128:256]
    cr, sr = t_ref[:, 256:384], t_ref[:, 384:512]

    c_ag = proj(_OFF_C, _OFF_QD)
    u_ref[...] = c_ag[:, :CONV_W] / (1.0 + jnp.exp(-c_ag[:, CONV_W:]))

    q = proj(_OFF_QD, _OFF_KD)
    for j in range(4):
        sl = slice(LANES * j, LANES * (j + 1))
        qkv_ref[:, sl] = (rope(q[:, sl], cd, sd, ROPE_ROT // 2) * (DIFF_DQK ** -0.5)).astype(BF16)
    k = proj(_OFF_KD, _OFF_VD)
    for j in range(4):
        sl = slice(LANES * j, LANES * (j + 1))
        kj = rope(k[:, sl], cd, sd, ROPE_ROT // 2)
        k_ref[:, sl] = kj
        qkv_ref[:, 512 + LANES * j:512 + LANES * (j + 1)] = kj.astype(BF16)
    v = proj(_OFF_VD, _OFF_QR)
    v_ref[...] = v
    qkv_ref[:, 1024:1536] = v.astype(BF16)

    r = proj(_OFF_QR, _OFF_GR)
    for j in range(2):
        sl = slice(LANES * j, LANES * (j + 1))
        ret_ref[:, sl] = rope(r[:, sl], cr, sr, RET_DK // 2).astype(BF16)
    for j in range(2, 4):
        sl = slice(LANES * j, LANES * (j + 1))
        ret_ref[:, sl] = (rope(r[:, sl], cr, sr, RET_DK // 2) * (RET_DK ** -0.5)).astype(BF16)
    ret_ref[:, 512:768] = r[:, 512:768].astype(BF16)
    gr_ref[...] = proj(_OFF_GR, _D_IN)


def _inproj(x2d, g, w_bf, table, tm):
    M, D = x2d.shape
    n_t = table.shape[0] // tm
    row = lambda i: (i, 0)
    return pl.pallas_call(
        _inproj_kernel,
        grid=(M // tm,),
        in_specs=[
            pl.BlockSpec((tm, D), row),
            pl.BlockSpec((1, D), lambda i: (0, 0)),
            pl.BlockSpec((D, _D_IN), lambda i: (0, 0)),
            pl.BlockSpec((tm, 512), lambda i: (i % n_t, 0)),
        ],
        out_specs=[
            pl.BlockSpec((tm, CONV_W), row),
            pl.BlockSpec((tm, 1536), row),
            pl.BlockSpec((tm, 512), row),
            pl.BlockSpec((tm, 512), row),
            pl.BlockSpec((tm, 768), row),
            pl.BlockSpec((tm, RET_W), row),
        ],
        out_shape=[
            jax.ShapeDtypeStruct((M, CONV_W), F32),
            jax.ShapeDtypeStruct((M, 1536), BF16),
            jax.ShapeDtypeStruct((M, 512), F32),
            jax.ShapeDtypeStruct((M, 512), F32),
            jax.ShapeDtypeStruct((M, 768), BF16),
            jax.ShapeDtypeStruct((M, RET_W), F32),
        ],
        compiler_params=_params("parallel"),
    )(x2d, g.reshape(1, D), w_bf, table)


def _attn_prompt_kernel(qi_ref, ki_ref, q_ref, k_ref, v_ref, lam_ref, sub_ref, o_ref,
                        qs, m_ref, l_ref, acc_ref, *, T, lam_init):
    t = pl.program_id(2)
    qi = qi_ref[t]
    ki = ki_ref[t]

    @pl.when(ki == 0)
    def _():
        q = q_ref[...]
        lane = lax.broadcasted_iota(jnp.int32, (T, LANES), 1)
        qs[0:T, :] = jnp.where(lane < DIFF_DQK, q, jnp.zeros_like(q))
        qs[T:2 * T, :] = jnp.where(lane >= DIFF_DQK, q, jnp.zeros_like(q))
        m_ref[...] = jnp.full(m_ref.shape, -jnp.inf, F32)
        l_ref[...] = jnp.zeros(l_ref.shape, F32)
        acc_ref[...] = jnp.zeros(acc_ref.shape, F32)

    def step(masked):
        s = lax.dot_general(qs[...], k_ref[...], _NT, preferred_element_type=F32)
        if masked:
            row = lax.broadcasted_iota(jnp.int32, (2 * T, T), 0) % T
            col = lax.broadcasted_iota(jnp.int32, (2 * T, T), 1)
            s = jnp.where(col <= row, s, -jnp.inf)
        m_prev = m_ref[...]
        m_new = jnp.maximum(m_prev, jnp.max(s, axis=-1, keepdims=True))
        a = jnp.exp(m_prev - m_new)
        p = jnp.exp(s - m_new)
        l_ref[...] = a * l_ref[...] + jnp.sum(p, axis=-1, keepdims=True)
        acc_ref[...] = a * acc_ref[...] + jnp.dot(p.astype(BF16), v_ref[...],
                                                  preferred_element_type=F32)
        m_ref[...] = m_new

    @pl.when(ki < qi)
    def _():
        step(False)

    @pl.when(ki == qi)
    def _():
        step(True)
        lam = _lam(lam_ref, lam_init)
        o = acc_ref[...] / l_ref[...]
        d = o[0:T, :] - lam * o[T:2 * T, :]
        o_ref[...] = (_rms(d, sub_ref[...]) * (1.0 - lam_init)).astype(BF16)


def _attn_prompt(qkv, lam4, subln, lam_init, B, S, T):
    M = qkv.shape[0]
    nq = S // T
    qi = jnp.asarray([q for q in range(nq) for _ in range(q + 1)], jnp.int32)
    ki = jnp.asarray([k for q in range(nq) for k in range(q + 1)], jnp.int32)
    n_tri = nq * (nq + 1) // 2
    grid_spec = pltpu.PrefetchScalarGridSpec(
        num_scalar_prefetch=2,
        grid=(B, DIFF_H, n_tri),
        in_specs=[
            pl.BlockSpec((T, LANES), lambda b, h, t, qi, ki: (b * nq + qi[t], h)),
            pl.BlockSpec((T, LANES), lambda b, h, t, qi, ki: (b * nq + ki[t], DIFF_H + h)),
            pl.BlockSpec((T, LANES), lambda b, h, t, qi, ki: (b * nq + ki[t], 2 * DIFF_H + h)),
            pl.BlockSpec((4, DIFF_DQK), lambda b, h, t, qi, ki: (0, 0)),
            pl.BlockSpec((1, DIFF_DV), lambda b, h, t, qi, ki: (0, 0)),
        ],
        out_specs=pl.BlockSpec((T, LANES), lambda b, h, t, qi, ki: (b * nq + qi[t], h)),
        scratch_shapes=[
            pltpu.VMEM((2 * T, LANES), BF16),
            pltpu.VMEM((2 * T, 1), F32),
            pltpu.VMEM((2 * T, 1), F32),
            pltpu.VMEM((2 * T, DIFF_DV), F32),
        ],
    )
    return pl.pallas_call(
        functools.partial(_attn_prompt_kernel, T=T, lam_init=lam_init),
        grid_spec=grid_spec,
        out_shape=jax.ShapeDtypeStruct((M, DIFF_W), BF16),
        compiler_params=_params("parallel", "parallel", "arbitrary"),
    )(qi, ki, qkv, qkv, qkv, lam4, subln.reshape(1, DIFF_DV))


def _group_norm_gate(o, g, gn):
    hid = lax.broadcasted_iota(jnp.int32, (1, RET_W), 1) // RET_DV

    def group_mean(x):
        out = jnp.zeros_like(x)
        for h in range(RET_H):
            mh = hid == h
            s = jnp.sum(jnp.where(mh, x, 0.0), axis=-1, keepdims=True) * (1.0 / RET_DV)
            out = jnp.where(mh, s, out)
        return out

    d = o - group_mean(o)
    var = group_mean(d * d)
    return d * lax.rsqrt(var + EPS) * gn * _silu(g)


def _ret_prompt_kernel(q_ref, k_ref, v_ref, g_ref, gn_ref, dec_ref, cross_ref, tail_ref, gl_ref,
                       o_ref, s_out_ref, S_ref):
    c = pl.program_id(1)

    @pl.when(c == 0)
    def _():
        S_ref[...] = jnp.zeros(S_ref.shape, F32)

    q = q_ref[...]
    k = k_ref[...]
    v = v_ref[...]
    hid = lax.broadcasted_iota(jnp.int32, (1, RET_W), 1) // RET_DV
    Sb = S_ref[...]
    o = jnp.dot(q, Sb.astype(BF16), preferred_element_type=F32) * cross_ref[...]
    for h in range(RET_H):
        mh = hid == h
        qh = jnp.where(mh, q, jnp.zeros_like(q))
        sc = lax.dot_general(qh, k, _NT, preferred_element_type=F32) * dec_ref[h]
        oh = jnp.dot(sc.astype(BF16), v, preferred_element_type=F32)
        o = o + jnp.where(mh, oh, 0.0)
    kt = (k.astype(F32) * tail_ref[...]).astype(BF16)
    upd = lax.dot_general(kt, v, (((0,), (0,)), ((), ())), preferred_element_type=F32)
    rowh = lax.broadcasted_iota(jnp.int32, (RET_W, RET_W), 0) // RET_DK
    colh = lax.broadcasted_iota(jnp.int32, (RET_W, RET_W), 1) // RET_DV
    S_new = gl_ref[...] * Sb + jnp.where(rowh == colh, upd, 0.0)
    S_ref[...] = S_new
    o_ref[...] = _group_norm_gate(o, g_ref[...], gn_ref[...]).astype(BF16)

    @pl.when(c == pl.num_programs(1) - 1)
    def _():
        y = S_new
        for sft in (64, 128, 192):
            y = y + pltpu.roll(S_new, sft, 1)
        s_out_ref[0] = y[:, 0:RET_DV]


def _ret_prompt(ret, gr, gn, B, S, L):
    M = ret.shape[0]
    nc = S // L
    decay, cross, tail, gl = _ret_tables(L)
    rep = lambda a: jnp.repeat(a, RET_DV, axis=-1)
    row = lambda b, c: (b * nc + c, 0)
    const2 = lambda b, c: (0, 0)
    o_r, s_fin = pl.pallas_call(
        _ret_prompt_kernel,
        grid=(B, nc),
        in_specs=[
            pl.BlockSpec((L, RET_W), lambda b, c: (b * nc + c, 0)),
            pl.BlockSpec((L, RET_W), lambda b, c: (b * nc + c, 1)),
            pl.BlockSpec((L, RET_W), lambda b, c: (b * nc + c, 2)),
            pl.BlockSpec((L, RET_W), row),
            pl.BlockSpec((1, RET_W), const2),
            pl.BlockSpec((RET_H, L, L), lambda b, c: (0, 0, 0)),
            pl.BlockSpec((L, RET_W), const2),
            pl.BlockSpec((L, RET_W), const2),
            pl.BlockSpec((1, RET_W), const2),
        ],
        out_specs=[
            pl.BlockSpec((L, RET_W), row),
            pl.BlockSpec((1, RET_W, RET_DV), lambda b, c: (b, 0, 0)),
        ],
        out_shape=[
            jax.ShapeDtypeStruct((M, RET_W), BF16),
            jax.ShapeDtypeStruct((B, RET_W, RET_DV), F32),
        ],
        scratch_shapes=[pltpu.VMEM((RET_W, RET_W), F32)],
        compiler_params=_params("parallel", "arbitrary"),
    )(ret, ret, ret, gr, gn.reshape(1, RET_W), decay, rep(cross), rep(tail), rep(gl[None, :]))
    return o_r, s_fin.reshape(B, RET_H, RET_DK, RET_DV)


def _conv_finish(acc, b, g, bb):
    y = acc + b
    mu = jnp.mean(y, axis=-1, keepdims=True)
    d = y - mu
    var = jnp.mean(d * d, axis=-1, keepdims=True)
    return _silu(d * lax.rsqrt(var + EPS) * g + bb)


def _conv_prompt_kernel(prev_ref, cur_ref, w_ref, b_ref, g_ref, bb_ref, o_ref, buf, *, tc, sub):
    i = pl.program_id(1)
    prev = prev_ref[...]
    buf[0:CONV_HALO, :] = jnp.where(i == 0, jnp.zeros_like(prev), prev)
    buf[CONV_HALO:CONV_HALO + tc, :] = cur_ref[...]
    off = CONV_HALO - (CONV_K - 1)

    for r0 in range(0, tc, sub):
        acc = jnp.zeros((sub, CONV_W), F32)
        for j in range(CONV_K):
            acc = acc + buf[r0 + off + j:r0 + off + j + sub, :] * w_ref[j:j + 1, :]
        o_ref[r0:r0 + sub, :] = _conv_finish(acc, b_ref[...], g_ref[...], bb_ref[...]).astype(BF16)


def _conv_prompt(u, w, b, g, bb, B, S, tc, sub=64):
    M = u.shape[0]
    nb = S // tc
    hb = tc // CONV_HALO
    vec = lambda a: a.reshape(1, CONV_W)
    const2 = lambda bi, i: (0, 0)
    return pl.pallas_call(
        functools.partial(_conv_prompt_kernel, tc=tc, sub=sub),
        grid=(B, nb),
        in_specs=[
            pl.BlockSpec((CONV_HALO, CONV_W),
                         lambda bi, i: (jnp.maximum((bi * nb + i) * hb - 1, 0), 0)),
            pl.BlockSpec((tc, CONV_W), lambda bi, i: (bi * nb + i, 0)),
            pl.BlockSpec((CONV_HALO, CONV_W), const2),
            pl.BlockSpec((1, CONV_W), const2),
            pl.BlockSpec((1, CONV_W), const2),
            pl.BlockSpec((1, CONV_W), const2),
        ],
        out_specs=pl.BlockSpec((tc, CONV_W), lambda bi, i: (bi * nb + i, 0)),
        out_shape=jax.ShapeDtypeStruct((M, CONV_W), BF16),
        scratch_shapes=[pltpu.VMEM((CONV_HALO + tc, CONV_W), F32)],
        compiler_params=_params("parallel", "arbitrary"),
    )(u, u, jnp.pad(w, ((0, CONV_HALO - CONV_K), (0, 0))), vec(b), vec(g), vec(bb))


def _outproj_mlp_kernel(x_ref, c_ref, od_ref, or_ref, wo_ref, g_ref, wu_ref, wd_ref, gf_ref,
                        o_ref, acc_ref, h2_ref, *, final_norm):
    f = pl.program_id(1)

    @pl.when(f == 0)
    def _():
        mix = jnp.dot(c_ref[...].astype(BF16), wo_ref[0:CONV_W, :], preferred_element_type=F32)
        mix += jnp.dot(od_ref[...].astype(BF16), wo_ref[CONV_W:CONV_W + DIFF_W, :],
                       preferred_element_type=F32)
        mix += jnp.dot(or_ref[...].astype(BF16), wo_ref[CONV_W + DIFF_W:, :],
                       preferred_element_type=F32)
        xm = x_ref[...] + mix
        acc_ref[...] = xm
        h2_ref[...] = _rms(xm, g_ref[...]).astype(BF16)

    up = jnp.dot(h2_ref[...], wu_ref[...], preferred_element_type=F32)
    a = jnp.square(jnp.maximum(up, 0.0)).astype(BF16)
    acc_ref[...] += jnp.dot(a, wd_ref[...], preferred_element_type=F32)

    @pl.when(f == pl.num_programs(1) - 1)
    def _():
        y = acc_ref[...]
        if final_norm:
            y = _rms(y, gf_ref[...])
        o_ref[...] = y


def _outproj_mlp(x2d, c, od, orr, wo_bf, g, wu_bf, wd_bf, gf, final_norm, tm, tf):
    M, D = x2d.shape
    FF = wu_bf.shape[1]
    row = lambda i, f: (i, 0)
    const2 = lambda i, f: (0, 0)
    return pl.pallas_call(
        functools.partial(_outproj_mlp_kernel, final_norm=final_norm),
        grid=(M // tm, FF // tf),
        in_specs=[
            pl.BlockSpec((tm, D), row),
            pl.BlockSpec((tm, CONV_W), row),
            pl.BlockSpec((tm, DIFF_W), row),
            pl.BlockSpec((tm, RET_W), row),
            pl.BlockSpec((D, D), const2),
            pl.BlockSpec((1, D), const2),
            pl.BlockSpec((D, tf), lambda i, f: (0, f)),
            pl.BlockSpec((tf, D), lambda i, f: (f, 0)),
            pl.BlockSpec((1, D), const2),
        ],
        out_specs=pl.BlockSpec((tm, D), row),
        out_shape=jax.ShapeDtypeStruct((M, D), F32),
        scratch_shapes=[pltpu.VMEM((tm, D), F32), pltpu.VMEM((tm, D), BF16)],
        compiler_params=_params("parallel", "arbitrary"),
    )(x2d, c, od, orr, wo_bf, g.reshape(1, D), wu_bf, wd_bf, gf.reshape(1, D))


def _attn_sample_kernel(pt_ref, q_ref, kn_ref, vn_ref, lam_ref, sub_ref, *rest,
                        P, TD, lam_init):
    kp = rest[0:P]
    vp = rest[P:2 * P]
    o_ref, wq, m_ref, l_ref, acc_ref = rest[2 * P:]
    R = 8 * TD
    s_id = pl.program_id(1)

    @pl.when(s_id == 0)
    def _():
        q = q_ref[0]
        hm = lax.broadcasted_iota(jnp.int32, (8, 512), 0)
        lh = lax.broadcasted_iota(jnp.int32, (8, 512), 1) // DIFF_DQK
        for t in range(TD):
            wq[8 * t:8 * t + 8, :] = jnp.where(hm == lh, jnp.broadcast_to(q[t:t + 1, :], (8, 512)), 0.0)
        m_ref[...] = jnp.full(m_ref.shape, -jnp.inf, F32)
        l_ref[...] = jnp.zeros(l_ref.shape, F32)
        acc_ref[...] = jnp.zeros(acc_ref.shape, F32)

    def update(s, vmat):
        m_prev = m_ref[...]
        m_new = jnp.maximum(m_prev, jnp.max(s, axis=-1, keepdims=True))
        a = jnp.exp(m_prev - m_new)
        p = jnp.exp(s - m_new)
        l_ref[...] = a * l_ref[...] + jnp.sum(p, axis=-1, keepdims=True)
        acc_ref[...] = a * acc_ref[...] + jnp.dot(p.astype(BF16), vmat, preferred_element_type=F32)
        m_ref[...] = m_new

    wqb = wq[...].astype(BF16)
    kcat = jnp.concatenate([r[0].astype(BF16) for r in kp], axis=0)
    vcat = jnp.concatenate([r[0].astype(BF16) for r in vp], axis=0)
    update(lax.dot_general(wqb, kcat, _NT, preferred_element_type=F32), vcat)

    @pl.when(s_id == pl.num_programs(1) - 1)
    def _():
        pad = jnp.zeros((16 - TD, 512), F32)
        kn = jnp.concatenate([kn_ref[0], pad], axis=0).astype(BF16)
        vn = jnp.concatenate([vn_ref[0], pad], axis=0).astype(BF16)
        s2 = lax.dot_general(wqb, kn, _NT, preferred_element_type=F32)
        tq = lax.broadcasted_iota(jnp.int32, (R, 16), 0) // 8
        tk = lax.broadcasted_iota(jnp.int32, (R, 16), 1)
        update(jnp.where(tk <= tq, s2, -jnp.inf), vn)

        lam = _lam(lam_ref, lam_init)
        o = acc_ref[...] / l_ref[...]
        hm = lax.broadcasted_iota(jnp.int32, (8, 512), 0)
        lh = lax.broadcasted_iota(jnp.int32, (8, 512), 1) // DIFF_DV
        coef = jnp.where(hm // 2 == lh, jnp.where(hm % 2 == 0, 1.0, -lam), 0.0)
        for t in range(TD):
            d = jnp.sum(o[8 * t:8 * t + 8, :] * coef, axis=0, keepdims=True)
            for h in range(DIFF_H):
                sl = slice(DIFF_DV * h, DIFF_DV * (h + 1))
                o_ref[0, t:t + 1, sl] = _rms(d[:, sl], sub_ref[...]) * (1.0 - lam_init)


def _attn_sample(qkv3, cache_k3, cache_v3, pt_flat, lam4, subln, lam_init, layer, n_phys, n_pages, P):
    DB, TD, _ = qkv3.shape
    page = cache_k3.shape[1]
    base = layer * n_phys

    def page_map(j):
        return lambda b, s, pt: (base + pt[b * n_pages + s * P + j], 0, 0)

    in_specs = [
        pl.BlockSpec((1, TD, 512), lambda b, s, pt: (b, 0, 0)),
        pl.BlockSpec((1, TD, 512), lambda b, s, pt: (b, 0, 1)),
        pl.BlockSpec((1, TD, 512), lambda b, s, pt: (b, 0, 2)),
        pl.BlockSpec((4, DIFF_DQK), lambda b, s, pt: (0, 0)),
        pl.BlockSpec((1, DIFF_DV), lambda b, s, pt: (0, 0)),
    ]
    in_specs += [pl.BlockSpec((1, page, 512), page_map(j)) for j in range(P)]
    in_specs += [pl.BlockSpec((1, page, 512), page_map(j)) for j in range(P)]
    R = 8 * TD
    grid_spec = pltpu.PrefetchScalarGridSpec(
        num_scalar_prefetch=1,
        grid=(DB, n_pages // P),
        in_specs=in_specs,
        out_specs=pl.BlockSpec((1, TD, 512), lambda b, s, pt: (b, 0, 0)),
        scratch_shapes=[
            pltpu.VMEM((R, 512), F32),
            pltpu.VMEM((R, 1), F32),
            pltpu.VMEM((R, 1), F32),
            pltpu.VMEM((R, 512), F32),
        ],
    )
    return pl.pallas_call(
        functools.partial(_attn_sample_kernel, P=P, TD=TD, lam_init=lam_init),
        grid_spec=grid_spec,
        out_shape=jax.ShapeDtypeStruct((DB, TD, 512), F32),
        compiler_params=_params("parallel", "arbitrary"),
    )(pt_flat, qkv3, qkv3, qkv3, lam4, subln.reshape(1, DIFF_DV),
      *([cache_k3] * P), *([cache_v3] * P))


def _ret_sample_kernel(q_ref, k_ref, kT_ref, v_ref, g_ref, gn_ref, S_ref, dec_ref, cross_ref,
                       tail_ref, gl_ref, o_ref, s_out_ref, *, TD):
    q = q_ref[...]
    k = k_ref[...]
    v = v_ref[...]
    S = S_ref[...]
    sc = jnp.einsum('gtd,gsd->gts', q, k, preferred_element_type=F32) * dec_ref[...]
    o = jnp.einsum('gtd,gde->gte', q, S, preferred_element_type=F32) * cross_ref[...]
    for s in range(TD):
        o = o + sc[:, :, s:s + 1] * v[:, s:s + 1, :]
    kT = kT_ref[...]
    vt = v * tail_ref[...]
    S_new = gl_ref[...] * S
    for s in range(TD):
        S_new = S_new + kT[:, :, s:s + 1] * vt[:, s:s + 1, :]
    s_out_ref[...] = S_new
    mu = jnp.mean(o, axis=-1, keepdims=True)
    d = o - mu
    var = jnp.mean(d * d, axis=-1, keepdims=True)
    o_ref[...] = d * lax.rsqrt(var + EPS) * gn_ref[...] * _silu(g_ref[...])


def _ret_sample(q, k, v, g, gn, S_prev, TD, Gb):
    G = q.shape[0]
    lg = jnp.tile(_log_gamma(), G // RET_H)
    decay, cross, tail, gl = _ret_tables(TD)
    dec_g = jnp.tile(decay, (G // RET_H, 1, 1))
    cross_g = jnp.tile(cross.T, (G // RET_H, 1))[:, :, None]
    tail_g = jnp.tile(tail.T, (G // RET_H, 1))[:, :, None]
    gl_g = jnp.tile(gl, G // RET_H)[:, None, None]
    del lg
    kT = jnp.swapaxes(k, 1, 2)
    b3 = lambda shp: pl.BlockSpec((Gb,) + shp, lambda i: (i, 0, 0))
    return pl.pallas_call(
        functools.partial(_ret_sample_kernel, TD=TD),
        grid=(G // Gb,),
        in_specs=[b3((TD, 64)), b3((TD, 64)), b3((64, TD)), b3((TD, 64)), b3((TD, 64)), b3((1, 64)),
                  b3((64, 64)), b3((TD, TD)), b3((TD, 1)), b3((TD, 1)), b3((1, 1))],
        out_specs=[b3((TD, 64)), b3((64, 64))],
        out_shape=[jax.ShapeDtypeStruct((G, TD, 64), F32), jax.ShapeDtypeStruct((G, 64, 64), F32)],
        compiler_params=_params("parallel"),
    )(q, k, kT, v, g, gn, S_prev, dec_g, cross_g, tail_g, gl_g)


def _conv_sample_kernel(ext_ref, w_ref, b_ref, g_ref, bb_ref, o_ref, *, TD):
    acc = jnp.zeros(o_ref.shape, F32)
    for j in range(CONV_K):
        acc = acc + ext_ref[:, j:j + TD, :] * w_ref[j:j + 1, :]
    o_ref[...] = _conv_finish(acc, b_ref[...], g_ref[...], bb_ref[...])


def _conv_sample(ext, w, b, g, bb, TD):
    DB = ext.shape[0]
    vec = lambda a: a.reshape(1, CONV_W)
    return pl.pallas_call(
        functools.partial(_conv_sample_kernel, TD=TD),
        out_shape=jax.ShapeDtypeStruct((DB, TD, CONV_W), F32),
    )(ext, w, vec(b), vec(g), vec(bb))


def _pick(n, pref):
    t = min(n, pref)
    while n % t:
        t //= 2
    return t


def kernel(x_prompt, x_sample, cache_k, cache_v, state_conv, state_ret, page_table, norm_mix, w_in, conv_w, conv_b, conv_ln_g, conv_ln_b, lam_q1, lam_k1, lam_q2, lam_k2, diff_subln, ret_gn, w_out, norm_mlp, w_up, w_down, norm_final):
    B, S, D = x_prompt.shape
    DB, TD, _ = x_sample.shape
    depth = w_in.shape[0]
    n_phys, page = cache_k.shape[1], cache_k.shape[2]
    n_pages = page_table.shape[1]
    past_len = n_pages * page
    Mp, Ms = B * S, DB * TD

    tm_in = _pick(S, 512)
    T_attn = _pick(S, 512)
    L_ret = _pick(S, 256)
    tc_conv = _pick(S, 512)
    tm_mlp = _pick(Mp, 1024)
    tf_mlp = _pick(w_up.shape[2], 512)
    P = _pick(n_pages, 8)
    Gb = _pick(DB * RET_H, 8)

    w_in_bf = w_in.astype(BF16)
    w_out_bf = w_out.astype(BF16)
    w_up_bf = w_up.astype(BF16)
    w_down_bf = w_down.astype(BF16)

    tab_p = _rope_table(jnp.arange(S))
    tab_s = jnp.tile(_rope_table(past_len + jnp.arange(TD)), (DB, 1))
    lam_all = jnp.stack([lam_q1, lam_k1, lam_q2, lam_k2], axis=1)
    cache_k3 = cache_k.reshape(depth * n_phys, page, 2 * DIFF_H * DIFF_DQK)
    cache_v3 = cache_v.reshape(depth * n_phys, page, DIFF_W)
    pt_flat = page_table.reshape(-1)

    xp = x_prompt.reshape(Mp, D)
    xs = x_sample.reshape(Ms, D)
    kp, vp, cp, rp, ks_, vs_, cs_, rs_ = [], [], [], [], [], [], [], []
    for l in range(depth):
        lam_init = 0.8 - 0.6 * math.exp(-0.3 * l)
        last = l == depth - 1

        u, qkv, k_d, v_d, ret, g_r = _inproj(xp, norm_mix[l], w_in_bf[l], tab_p, tm_in)
        o_d = _attn_prompt(qkv, lam_all[l], diff_subln[l], lam_init, B, S, T_attn)
        o_r, s_fin = _ret_prompt(ret, g_r, ret_gn[l], B, S, L_ret)
        c = _conv_prompt(u, conv_w[l], conv_b[l], conv_ln_g[l], conv_ln_b[l], B, S, tc_conv)
        xp = _outproj_mlp(xp, c, o_d, o_r, w_out_bf[l], norm_mlp[l], w_up_bf[l], w_down_bf[l],
                          norm_final, last, tm_mlp, tf_mlp)
        kp.append(k_d.reshape(B, S, 2 * DIFF_H, DIFF_DQK))
        vp.append(v_d.reshape(B, S, DIFF_H, DIFF_DV))
        cp.append(u.reshape(B, S, CONV_W)[:, S - (CONV_K - 1):])
        rp.append(s_fin)

        u, qkv, k_d, v_d, ret, g_r = _inproj(xs, norm_mix[l], w_in_bf[l], tab_s, Ms)
        qkv3 = qkv.astype(F32).reshape(DB, TD, 1536)
        o_d = _attn_sample(qkv3, cache_k3, cache_v3, pt_flat, lam_all[l], diff_subln[l], lam_init,
                           l, n_phys, n_pages, P)
        to_g = lambda a: a.astype(F32).reshape(DB, TD, RET_H, 64).swapaxes(1, 2).reshape(
            DB * RET_H, TD, 64)
        gn_g = jnp.tile(ret_gn[l].reshape(RET_H, 1, RET_DV), (DB, 1, 1))
        o_r, s_new = _ret_sample(to_g(ret[:, 0:256]), to_g(ret[:, 256:512]), to_g(ret[:, 512:768]),
                                 to_g(g_r), gn_g, state_ret[l].reshape(DB * RET_H, RET_DK, RET_DV),
                                 TD, Gb)
        o_r = o_r.reshape(DB, RET_H, TD, RET_DV).swapaxes(1, 2).reshape(Ms, RET_W)
        ext = jnp.concatenate([state_conv[l], u.reshape(DB, TD, CONV_W)], axis=1)
        c = _conv_sample(ext, conv_w[l], conv_b[l], conv_ln_g[l], conv_ln_b[l], TD)
        xs = _outproj_mlp(xs, c.reshape(Ms, CONV_W), o_d.reshape(Ms, DIFF_W), o_r, w_out_bf[l],
                          norm_mlp[l], w_up_bf[l], w_down_bf[l], norm_final, last, Ms, tf_mlp)
        ks_.append(k_d.reshape(DB, TD, 2 * DIFF_H, DIFF_DQK))
        vs_.append(v_d.reshape(DB, TD, DIFF_H, DIFF_DV))
        cs_.append(ext[:, TD:])
        rs_.append(s_new.reshape(DB, RET_H, RET_DK, RET_DV))

    return (xp.reshape(B, S, D), xs.reshape(DB, TD, D),
            jnp.stack(kp), jnp.stack(vp), jnp.stack(cp), jnp.stack(rp),
            jnp.stack(ks_), jnp.stack(vs_), jnp.stack(cs_), jnp.stack(rs_))
```

```python
import functools
import math

import jax
import jax.numpy as jnp
from jax import lax
from jax.experimental import pallas as pl
from jax.experimental.pallas import tpu as pltpu

F32 = jnp.float32
BF16 = jnp.bfloat16

CONV_W = 256
CONV_K = 31
DIFF_H = 4
DIFF_DQK = 64
DIFF_DV = 128
DIFF_W = DIFF_H * DIFF_DV
RET_H = 4
RET_DK = 64
RET_DV = 64
RET_W = RET_H * RET_DV
ROPE_THETA = 500000.0
ROPE_ROT = DIFF_DQK // 4
RET_THETA = 10000.0
EPS = 1e-5
LANES = 128
CONV_HALO = 32
VMEM_LIMIT = 56 * 1024 * 1024

_OFF_C, _OFF_QD, _OFF_KD, _OFF_VD, _OFF_QR, _OFF_KR, _OFF_VR, _OFF_GR, _D_IN = (
    0, 512, 1024, 1536, 2048, 2304, 2560, 2816, 3072)

_NT = (((1,), (1,)), ((), ()))


def _params(*sem):
    return pltpu.CompilerParams(dimension_semantics=sem, vmem_limit_bytes=VMEM_LIMIT)


def _rms(x, g):
    return x * lax.rsqrt(jnp.mean(x * x, axis=-1, keepdims=True) + EPS) * g


def _silu(x):
    return x / (1.0 + jnp.exp(-x))


def _lam(lam_ref, lam_init):
    s1 = jnp.sum(lam_ref[0:1, :] * lam_ref[1:2, :], axis=-1, keepdims=True)
    s2 = jnp.sum(lam_ref[2:3, :] * lam_ref[3:4, :], axis=-1, keepdims=True)
    return jnp.exp(s1) - jnp.exp(s2) + lam_init


def _rope_table(pos):
    posf = pos.astype(F32)[:, None]
    n = pos.shape[0]

    def one(n_rot, theta, hd):
        half = n_rot // 2
        inv = 1.0 / (theta ** (jnp.arange(half, dtype=F32) * 2.0 / n_rot))
        ang = posf * inv[None, :]
        cos, sin = jnp.cos(ang), jnp.sin(ang)
        c = jnp.concatenate([cos, cos, jnp.ones((n, hd - n_rot), F32)], axis=-1)
        s = jnp.concatenate([-sin, sin, jnp.zeros((n, hd - n_rot), F32)], axis=-1)
        return jnp.tile(c, (1, LANES // hd)), jnp.tile(s, (1, LANES // hd))

    cd, sd = one(ROPE_ROT, ROPE_THETA, DIFF_DQK)
    cr, sr = one(RET_DK, RET_THETA, RET_DK)
    return jnp.concatenate([cd, sd, cr, sr], axis=-1)


def _log_gamma():
    return jnp.log(1.0 - jnp.exp2(-5.0 - jnp.arange(RET_H, dtype=F32)))


def _ret_tables(L):
    lg = _log_gamma()
    idx = jnp.arange(L, dtype=F32)
    diff = idx[:, None] - idx[None, :]
    decay = jnp.where(diff >= 0, jnp.exp(jnp.maximum(diff, 0.0)[None] * lg[:, None, None]), 0.0)
    cross = jnp.exp((idx[:, None] + 1.0) * lg[None, :])
    tail = jnp.exp((L - 1.0 - idx)[:, None] * lg[None, :])
    gl = jnp.exp(L * lg)
    return decay, cross, tail, gl


def _inproj_kernel(x_ref, g_ref, w_ref, wvt_ref, t_ref, u_ref, qk_ref, k_ref, v_ref, vt_ref,
                   ret_ref, gr_ref):
    h = _rms(x_ref[...], g_ref[...]).astype(BF16)

    def proj(a, b):
        return jnp.dot(h, w_ref[:, a:b], preferred_element_type=F32)

    lane = lax.broadcasted_iota(jnp.int32, (1, LANES), 1) % 64

    def rope(xg, c, s, half):
        partner = jnp.where(lane < half, pltpu.roll(xg, LANES - half, 1), pltpu.roll(xg, half, 1))
        return xg * c + partner * s

    cd, sd = t_ref[:, 0:128], t_ref[:, 128:256]
    cr, sr = t_ref[:, 256:384], t_ref[:, 384:512]

    c_ag = proj(_OFF_C, _OFF_QD)
    u_ref[...] = c_ag[:, :CONV_W] / (1.0 + jnp.exp(-c_ag[:, CONV_W:]))

    q = proj(_OFF_QD, _OFF_KD)
    for j in range(4):
        sl = slice(LANES * j, LANES * (j + 1))
        qk_ref[:, sl] = (rope(q[:, sl], cd, sd, ROPE_ROT // 2) * (DIFF_DQK ** -0.5)).astype(BF16)
    k = proj(_OFF_KD, _OFF_VD)
    for j in range(4):
        sl = slice(LANES * j, LANES * (j + 1))
        kj = rope(k[:, sl], cd, sd, ROPE_ROT // 2)
        k_ref[:, sl] = kj
        qk_ref[:, 512 + LANES * j:512 + LANES * (j + 1)] = kj.astype(BF16)
    v_ref[...] = proj(_OFF_VD, _OFF_QR)
    vt_ref[...] = lax.dot_general(wvt_ref[...], h, _NT, preferred_element_type=F32).astype(BF16)

    r = proj(_OFF_QR, _OFF_GR)
    for j in range(2):
        sl = slice(LANES * j, LANES * (j + 1))
        ret_ref[:, sl] = rope(r[:, sl], cr, sr, RET_DK // 2).astype(BF16)
    for j in range(2, 4):
        sl = slice(LANES * j, LANES * (j + 1))
        ret_ref[:, sl] = (rope(r[:, sl], cr, sr, RET_DK // 2) * (RET_DK ** -0.5)).astype(BF16)
    ret_ref[:, 512:768] = r[:, 512:768].astype(BF16)
    gr_ref[...] = proj(_OFF_GR, _D_IN)


def _inproj(x2d, g, w_bf, wvt_bf, table, tm):
    M, D = x2d.shape
    n_t = table.shape[0] // tm
    row = lambda i: (i, 0)
    return pl.pallas_call(
        _inproj_kernel,
        grid=(M // tm,),
        in_specs=[
            pl.BlockSpec((tm, D), row),
            pl.BlockSpec((1, D), lambda i: (0, 0)),
            pl.BlockSpec((D, _D_IN), lambda i: (0, 0)),
            pl.BlockSpec((DIFF_W, D), lambda i: (0, 0)),
            pl.BlockSpec((tm, 512), lambda i: (i % n_t, 0)),
        ],
        out_specs=[
            pl.BlockSpec((tm, CONV_W), row),
            pl.BlockSpec((tm, 1024), row),
            pl.BlockSpec((tm, 512), row),
            pl.BlockSpec((tm, 512), row),
            pl.BlockSpec((DIFF_W, tm), lambda i: (0, i)),
            pl.BlockSpec((tm, 768), row),
            pl.BlockSpec((tm, RET_W), row),
        ],
        out_shape=[
            jax.ShapeDtypeStruct((M, CONV_W), F32),
            jax.ShapeDtypeStruct((M, 1024), BF16),
            jax.ShapeDtypeStruct((M, 512), F32),
            jax.ShapeDtypeStruct((M, 512), F32),
            jax.ShapeDtypeStruct((DIFF_W, M), BF16),
            jax.ShapeDtypeStruct((M, 768), BF16),
            jax.ShapeDtypeStruct((M, RET_W), F32),
        ],
        compiler_params=_params("parallel"),
    )(x2d, g.reshape(1, D), w_bf, wvt_bf, table)


def _attn_prompt_kernel(qi_ref, ki_ref, q_ref, k_ref, vt_ref, lam_ref, sub_ref, o_ref,
                        qs, m_ref, l_ref, acc_ref, *, T, lam_init):
    t = pl.program_id(2)
    qi = qi_ref[t]
    ki = ki_ref[t]

    @pl.when(ki == 0)
    def _():
        q = q_ref[...]
        lane = lax.broadcasted_iota(jnp.int32, (T, LANES), 1)
        qs[0] = jnp.where(lane < DIFF_DQK, q, jnp.zeros_like(q))
        qs[1] = jnp.where(lane >= DIFF_DQK, q, jnp.zeros_like(q))
        m_ref[...] = jnp.full(m_ref.shape, -jnp.inf, F32)
        l_ref[...] = jnp.zeros(l_ref.shape, F32)
        acc_ref[...] = jnp.zeros(acc_ref.shape, F32)

    def step(masked):
        k = k_ref[...]
        vt = vt_ref[...]
        for mp in range(2):
            s = lax.dot_general(k, qs[mp], _NT, preferred_element_type=F32)
            if masked:
                key = lax.broadcasted_iota(jnp.int32, (T, T), 0)
                qq = lax.broadcasted_iota(jnp.int32, (T, T), 1)
                s = jnp.where(key <= qq, s, -jnp.inf)
            m_prev = m_ref[mp]
            m_new = jnp.maximum(m_prev, jnp.max(s, axis=0, keepdims=True))
            a = jnp.exp(m_prev - m_new)
            p = jnp.exp(s - m_new)
            l_ref[mp] = a * l_ref[mp] + jnp.sum(p, axis=0, keepdims=True)
            acc_ref[mp] = a * acc_ref[mp] + jnp.dot(vt, p.astype(BF16),
                                                    preferred_element_type=F32)
            m_ref[mp] = m_new

    @pl.when(ki < qi)
    def _():
        step(False)

    @pl.when(ki == qi)
    def _():
        step(True)
        lam = _lam(lam_ref, lam_init)
        d = acc_ref[0] / l_ref[0] - lam * (acc_ref[1] / l_ref[1])
        r = lax.rsqrt(jnp.mean(d * d, axis=0, keepdims=True) + EPS)
        y = d * r * (sub_ref[...] * (1.0 - lam_init))
        o_ref[...] = y.T.astype(BF16)


def _attn_prompt(qk, vt, lam4, subln, lam_init, B, S, T):
    M = qk.shape[0]
    nq = S // T
    qi = jnp.asarray([q for q in range(nq) for _ in range(q + 1)], jnp.int32)
    ki = jnp.asarray([k for q in range(nq) for k in range(q + 1)], jnp.int32)
    n_tri = nq * (nq + 1) // 2
    grid_spec = pltpu.PrefetchScalarGridSpec(
        num_scalar_prefetch=2,
        grid=(B, DIFF_H, n_tri),
        in_specs=[
            pl.BlockSpec((T, LANES), lambda b, h, t, qi, ki: (b * nq + qi[t], h)),
            pl.BlockSpec((T, LANES), lambda b, h, t, qi, ki: (b * nq + ki[t], DIFF_H + h)),
            pl.BlockSpec((DIFF_DV, T), lambda b, h, t, qi, ki: (h, b * nq + ki[t])),
            pl.BlockSpec((4, DIFF_DQK), lambda b, h, t, qi, ki: (0, 0)),
            pl.BlockSpec((DIFF_DV, 1), lambda b, h, t, qi, ki: (0, 0)),
        ],
        out_specs=pl.BlockSpec((T, LANES), lambda b, h, t, qi, ki: (b * nq + qi[t], h)),
        scratch_shapes=[
            pltpu.VMEM((2, T, LANES), BF16),
            pltpu.VMEM((2, 1, T), F32),
            pltpu.VMEM((2, 1, T), F32),
            pltpu.VMEM((2, DIFF_DV, T), F32),
        ],
    )
    return pl.pallas_call(
        functools.partial(_attn_prompt_kernel, T=T, lam_init=lam_init),
        grid_spec=grid_spec,
        out_shape=jax.ShapeDtypeStruct((M, DIFF_W), BF16),
        compiler_params=_params("parallel", "parallel", "arbitrary"),
    )(qi, ki, qk, qk, vt, lam4, subln.reshape(DIFF_DV, 1))


def _group_norm_gate(o, g, gn):
    hid = lax.broadcasted_iota(jnp.int32, (1, RET_W), 1) // RET_DV

    def group_mean(x):
        out = jnp.zeros_like(x)
        for h in range(RET_H):
            mh = hid == h
            s = jnp.sum(jnp.where(mh, x, 0.0), axis=-1, keepdims=True) * (1.0 / RET_DV)
            out = jnp.where(mh, s, out)
        return out

    d = o - group_mean(o)
    var = group_mean(d * d)
    return d * lax.rsqrt(var + EPS) * gn * _silu(g)


def _ret_prompt_kernel(q_ref, k_ref, v_ref, g_ref, gn_ref, dec_ref, cross_ref, tail_ref, gl_ref,
                       o_ref, s_out_ref, S_ref):
    c = pl.program_id(1)

    @pl.when(c == 0)
    def _():
        S_ref[...] = jnp.zeros(S_ref.shape, F32)

    q = q_ref[...]
    k = k_ref[...]
    v = v_ref[...]
    hid = lax.broadcasted_iota(jnp.int32, (1, RET_W), 1) // RET_DV
    Sb = S_ref[...]
    o = jnp.dot(q, Sb.astype(BF16), preferred_element_type=F32) * cross_ref[...]
    for h in range(RET_H):
        mh = hid == h
        qh = jnp.where(mh, q, jnp.zeros_like(q))
        sc = lax.dot_general(qh, k, _NT, preferred_element_type=F32) * dec_ref[h]
        oh = jnp.dot(sc.astype(BF16), v, preferred_element_type=F32)
        o = o + jnp.where(mh, oh, 0.0)
    kt = (k.astype(F32) * tail_ref[...]).astype(BF16)
    upd = lax.dot_general(kt, v, (((0,), (0,)), ((), ())), preferred_element_type=F32)
    rowh = lax.broadcasted_iota(jnp.int32, (RET_W, RET_W), 0) // RET_DK
    colh = lax.broadcasted_iota(jnp.int32, (RET_W, RET_W), 1) // RET_DV
    S_new = gl_ref[...] * Sb + jnp.where(rowh == colh, upd, 0.0)
    S_ref[...] = S_new
    o_ref[...] = _group_norm_gate(o, g_ref[...], gn_ref[...]).astype(BF16)

    @pl.when(c == pl.num_programs(1) - 1)
    def _():
        y = S_new
        for sft in (64, 128, 192):
            y = y + pltpu.roll(S_new, sft, 1)
        s_out_ref[0] = y[:, 0:RET_DV]


def _ret_prompt(ret, gr, gn, B, S, L):
    M = ret.shape[0]
    nc = S // L
    decay, cross, tail, gl = _ret_tables(L)
    rep = lambda a: jnp.repeat(a, RET_DV, axis=-1)
    row = lambda b, c: (b * nc + c, 0)
    const2 = lambda b, c: (0, 0)
    o_r, s_fin = pl.pallas_call(
        _ret_prompt_kernel,
        grid=(B, nc),
        in_specs=[
            pl.BlockSpec((L, RET_W), lambda b, c: (b * nc + c, 0)),
            pl.BlockSpec((L, RET_W), lambda b, c: (b * nc + c, 1)),
            pl.BlockSpec((L, RET_W), lambda b, c: (b * nc + c, 2)),
            pl.BlockSpec((L, RET_W), row),
            pl.BlockSpec((1, RET_W), const2),
            pl.BlockSpec((RET_H, L, L), lambda b, c: (0, 0, 0)),
            pl.BlockSpec((L, RET_W), const2),
            pl.BlockSpec((L, RET_W), const2),
            pl.BlockSpec((1, RET_W), const2),
        ],
        out_specs=[
            pl.BlockSpec((L, RET_W), row),
            pl.BlockSpec((1, RET_W, RET_DV), lambda b, c: (b, 0, 0)),
        ],
        out_shape=[
            jax.ShapeDtypeStruct((M, RET_W), BF16),
            jax.ShapeDtypeStruct((B, RET_W, RET_DV), F32),
        ],
        scratch_shapes=[pltpu.VMEM((RET_W, RET_W), F32)],
        compiler_params=_params("parallel", "arbitrary"),
    )(ret, ret, ret, gr, gn.reshape(1, RET_W), decay, rep(cross), rep(tail), rep(gl[None, :]))
    return o_r, s_fin.reshape(B, RET_H, RET_DK, RET_DV)


def _conv_finish(acc, b, g, bb):
    y = acc + b
    mu = jnp.mean(y, axis=-1, keepdims=True)
    d = y - mu
    var = jnp.mean(d * d, axis=-1, keepdims=True)
    return _silu(d * lax.rsqrt(var + EPS) * g + bb)


def _conv_prompt_kernel(prev_ref, cur_ref, w_ref, b_ref, g_ref, bb_ref, o_ref, buf, sh, *, tc, sub):
    i = pl.program_id(1)
    prev = prev_ref[...]
    buf[0:CONV_HALO, :] = jnp.where(i == 0, jnp.zeros_like(prev), prev)
    buf[CONV_HALO:CONV_HALO + tc, :] = cur_ref[...]
    off = CONV_HALO - (CONV_K - 1)
    cls = [[j for j in range(CONV_K) if (off + j) % 8 == a] for a in range(8)]
    for a, taps in enumerate(cls):
        n = tc + taps[-1] - taps[0]
        sh[a, 0:n, :] = buf[off + taps[0]:off + taps[0] + n, :]

    for r0 in range(0, tc, sub):
        acc = jnp.zeros((sub, CONV_W), F32)
        for a, taps in enumerate(cls):
            for j in taps:
                acc = acc + sh[a, r0 + j - taps[0]:r0 + j - taps[0] + sub, :] * w_ref[j:j + 1, :]
        o_ref[r0:r0 + sub, :] = _conv_finish(acc, b_ref[...], g_ref[...], bb_ref[...]).astype(BF16)


def _conv_prompt(u, w, b, g, bb, B, S, tc, sub=64):
    M = u.shape[0]
    nb = S // tc
    hb = tc // CONV_HALO
    vec = lambda a: a.reshape(1, CONV_W)
    const2 = lambda bi, i: (0, 0)
    return pl.pallas_call(
        functools.partial(_conv_prompt_kernel, tc=tc, sub=sub),
        grid=(B, nb),
        in_specs=[
            pl.BlockSpec((CONV_HALO, CONV_W),
                         lambda bi, i: (jnp.maximum((bi * nb + i) * hb - 1, 0), 0)),
            pl.BlockSpec((tc, CONV_W), lambda bi, i: (bi * nb + i, 0)),
            pl.BlockSpec((CONV_HALO, CONV_W), const2),
            pl.BlockSpec((1, CONV_W), const2),
            pl.BlockSpec((1, CONV_W), const2),
            pl.BlockSpec((1, CONV_W), const2),
        ],
        out_specs=pl.BlockSpec((tc, CONV_W), lambda bi, i: (bi * nb + i, 0)),
        out_shape=jax.ShapeDtypeStruct((M, CONV_W), BF16),
        scratch_shapes=[pltpu.VMEM((CONV_HALO + tc, CONV_W), F32),
                        pltpu.VMEM((8, tc + CONV_HALO - 8, CONV_W), F32)],
        compiler_params=_params("parallel", "arbitrary"),
    )(u, u, jnp.pad(w, ((0, CONV_HALO - CONV_K), (0, 0))), vec(b), vec(g), vec(bb))


def _outproj_mlp_kernel(x_ref, c_ref, od_ref, or_ref, wo_ref, g_ref, wu_ref, wd_ref, gf_ref,
                        o_ref, acc_ref, h2_ref, *, final_norm):
    f = pl.program_id(1)

    @pl.when(f == 0)
    def _():
        mix = jnp.dot(c_ref[...].astype(BF16), wo_ref[0:CONV_W, :], preferred_element_type=F32)
        mix += jnp.dot(od_ref[...].astype(BF16), wo_ref[CONV_W:CONV_W + DIFF_W, :],
                       preferred_element_type=F32)
        mix += jnp.dot(or_ref[...].astype(BF16), wo_ref[CONV_W + DIFF_W:, :],
                       preferred_element_type=F32)
        xm = x_ref[...] + mix
        acc_ref[...] = xm
        h2_ref[...] = _rms(xm, g_ref[...]).astype(BF16)

    up = jnp.dot(h2_ref[...], wu_ref[...], preferred_element_type=F32)
    a = jnp.square(jnp.maximum(up, 0.0)).astype(BF16)
    acc_ref[...] += jnp.dot(a, wd_ref[...], preferred_element_type=F32)

    @pl.when(f == pl.num_programs(1) - 1)
    def _():
        y = acc_ref[...]
        if final_norm:
            y = _rms(y, gf_ref[...])
        o_ref[...] = y


def _outproj_mlp(x2d, c, od, orr, wo_bf, g, wu_bf, wd_bf, gf, final_norm, tm, tf):
    M, D = x2d.shape
    FF = wu_bf.shape[1]
    row = lambda i, f: (i, 0)
    const2 = lambda i, f: (0, 0)
    return pl.pallas_call(
        functools.partial(_outproj_mlp_kernel, final_norm=final_norm),
        grid=(M // tm, FF // tf),
        in_specs=[
            pl.BlockSpec((tm, D), row),
            pl.BlockSpec((tm, CONV_W), row),
            pl.BlockSpec((tm, DIFF_W), row),
            pl.BlockSpec((tm, RET_W), row),
            pl.BlockSpec((D, D), const2),
            pl.BlockSpec((1, D), const2),
            pl.BlockSpec((D, tf), lambda i, f: (0, f)),
            pl.BlockSpec((tf, D), lambda i, f: (f, 0)),
            pl.BlockSpec((1, D), const2),
        ],
        out_specs=pl.BlockSpec((tm, D), row),
        out_shape=jax.ShapeDtypeStruct((M, D), F32),
        scratch_shapes=[pltpu.VMEM((tm, D), F32), pltpu.VMEM((tm, D), BF16)],
        compiler_params=_params("parallel", "arbitrary"),
    )(x2d, c, od, orr, wo_bf, g.reshape(1, D), wu_bf, wd_bf, gf.reshape(1, D))


def _attn_sample_kernel(pt_ref, q_ref, kn_ref, vn_ref, lam_ref, sub_ref, *rest,
                        P, TD, page, lam_init):
    kp = rest[0:P]
    vp = rest[P:2 * P]
    o_ref, wq, m_ref, l_ref, acc_ref = rest[2 * P:]
    R = 8 * TD
    s_id = pl.program_id(1)

    @pl.when(s_id == 0)
    def _():
        q = q_ref[0]
        hm = lax.broadcasted_iota(jnp.int32, (8, 512), 0)
        lh = lax.broadcasted_iota(jnp.int32, (8, 512), 1) // DIFF_DQK
        for t in range(TD):
            wq[8 * t:8 * t + 8, :] = jnp.where(hm == lh, jnp.broadcast_to(q[t:t + 1, :], (8, 512)), 0.0)
        m_ref[...] = jnp.full(m_ref.shape, -jnp.inf, F32)
        l_ref[...] = jnp.zeros(l_ref.shape, F32)
        acc_ref[...] = jnp.zeros(acc_ref.shape, F32)

    def update(s, vmat):
        m_prev = m_ref[...]
        m_new = jnp.maximum(m_prev, jnp.max(s, axis=-1, keepdims=True))
        a = jnp.exp(m_prev - m_new)
        p = jnp.exp(s - m_new)
        l_ref[...] = a * l_ref[...] + jnp.sum(p, axis=-1, keepdims=True)
        acc_ref[...] = a * acc_ref[...] + jnp.dot(p.astype(BF16), vmat, preferred_element_type=F32)
        m_ref[...] = m_new

    wqb = wq[...].astype(BF16)
    kcat = jnp.concatenate([r[0].astype(BF16) for r in kp], axis=1)

    def v_page(r):
        heads = [r[0, pl.ds(h, page, stride=DIFF_H), :] for h in range(DIFF_H)]
        return jnp.concatenate(heads, axis=1).astype(BF16)

    vcat = jnp.concatenate([v_page(r) for r in vp], axis=0)
    update(jnp.dot(wqb, kcat, preferred_element_type=F32), vcat)

    @pl.when(s_id == pl.num_programs(1) - 1)
    def _():
        pad = jnp.zeros((16 - TD, 512), F32)
        kn = jnp.concatenate([kn_ref[0], pad], axis=0).astype(BF16)
        vn = jnp.concatenate([vn_ref[0], pad], axis=0).astype(BF16)
        s2 = lax.dot_general(wqb, kn, _NT, preferred_element_type=F32)
        tq = lax.broadcasted_iota(jnp.int32, (R, 16), 0) // 8
        tk = lax.broadcasted_iota(jnp.int32, (R, 16), 1)
        update(jnp.where(tk <= tq, s2, -jnp.inf), vn)

        lam = _lam(lam_ref, lam_init)
        o = acc_ref[...] / l_ref[...]
        hm = lax.broadcasted_iota(jnp.int32, (8, 512), 0)
        lh = lax.broadcasted_iota(jnp.int32, (8, 512), 1) // DIFF_DV
        coef = jnp.where(hm // 2 == lh, jnp.where(hm % 2 == 0, 1.0, -lam), 0.0)
        for t in range(TD):
            d = jnp.sum(o[8 * t:8 * t + 8, :] * coef, axis=0, keepdims=True)
            for h in range(DIFF_H):
                sl = slice(DIFF_DV * h, DIFF_DV * (h + 1))
                o_ref[0, t:t + 1, sl] = _rms(d[:, sl], sub_ref[...]) * (1.0 - lam_init)


def _attn_sample(q3, kn3, vn3, cache_kt3, cache_v3, pt_flat, lam4, subln, lam_init, layer, n_phys,
                 n_pages, P):
    DB, TD, _ = q3.shape
    page = cache_kt3.shape[2]
    base = layer * n_phys

    def page_map(j):
        return lambda b, s, pt: (base + pt[b * n_pages + s * P + j], 0, 0)

    tok = pl.BlockSpec((1, TD, 512), lambda b, s, pt: (b, 0, 0))
    in_specs = [
        tok, tok, tok,
        pl.BlockSpec((4, DIFF_DQK), lambda b, s, pt: (0, 0)),
        pl.BlockSpec((1, DIFF_DV), lambda b, s, pt: (0, 0)),
    ]
    in_specs += [pl.BlockSpec((1, 512, page), page_map(j)) for j in range(P)]
    in_specs += [pl.BlockSpec((1, page * DIFF_H, DIFF_DV), page_map(j)) for j in range(P)]
    R = 8 * TD
    grid_spec = pltpu.PrefetchScalarGridSpec(
        num_scalar_prefetch=1,
        grid=(DB, n_pages // P),
        in_specs=in_specs,
        out_specs=pl.BlockSpec((1, TD, 512), lambda b, s, pt: (b, 0, 0)),
        scratch_shapes=[
            pltpu.VMEM((R, 512), F32),
            pltpu.VMEM((R, 1), F32),
            pltpu.VMEM((R, 1), F32),
            pltpu.VMEM((R, 512), F32),
        ],
    )
    return pl.pallas_call(
        functools.partial(_attn_sample_kernel, P=P, TD=TD, page=page, lam_init=lam_init),
        grid_spec=grid_spec,
        out_shape=jax.ShapeDtypeStruct((DB, TD, 512), F32),
        compiler_params=_params("parallel", "arbitrary"),
    )(pt_flat, q3, kn3, vn3, lam4, subln.reshape(1, DIFF_DV),
      *([cache_kt3] * P), *([cache_v3] * P))


def _ret_sample_kernel(q_ref, k_ref, kT_ref, v_ref, g_ref, gn_ref, S_ref, dec_ref, cross_ref,
                       tail_ref, gl_ref, o_ref, s_out_ref, *, TD):
    q = q_ref[...]
    k = k_ref[...]
    v = v_ref[...]
    S = S_ref[...]
    sc = jnp.einsum('gtd,gsd->gts', q, k, preferred_element_type=F32) * dec_ref[...]
    o = jnp.einsum('gtd,gde->gte', q, S, preferred_element_type=F32) * cross_ref[...]
    for s in range(TD):
        o = o + sc[:, :, s:s + 1] * v[:, s:s + 1, :]
    kT = kT_ref[...]
    vt = v * tail_ref[...]
    S_new = gl_ref[...] * S
    for s in range(TD):
        S_new = S_new + kT[:, :, s:s + 1] * vt[:, s:s + 1, :]
    s_out_ref[...] = S_new
    mu = jnp.mean(o, axis=-1, keepdims=True)
    d = o - mu
    var = jnp.mean(d * d, axis=-1, keepdims=True)
    o_ref[...] = d * lax.rsqrt(var + EPS) * gn_ref[...] * _silu(g_ref[...])


def _ret_sample(q, k, v, g, gn, S_prev, TD, Gb):
    G = q.shape[0]
    lg = jnp.tile(_log_gamma(), G // RET_H)
    decay, cross, tail, gl = _ret_tables(TD)
    dec_g = jnp.tile(decay, (G // RET_H, 1, 1))
    cross_g = jnp.tile(cross.T, (G // RET_H, 1))[:, :, None]
    tail_g = jnp.tile(tail.T, (G // RET_H, 1))[:, :, None]
    gl_g = jnp.tile(gl, G // RET_H)[:, None, None]
    del lg
    kT = jnp.swapaxes(k, 1, 2)
    b3 = lambda shp: pl.BlockSpec((Gb,) + shp, lambda i: (i, 0, 0))
    return pl.pallas_call(
        functools.partial(_ret_sample_kernel, TD=TD),
        grid=(G // Gb,),
        in_specs=[b3((TD, 64)), b3((TD, 64)), b3((64, TD)), b3((TD, 64)), b3((TD, 64)), b3((1, 64)),
                  b3((64, 64)), b3((TD, TD)), b3((TD, 1)), b3((TD, 1)), b3((1, 1))],
        out_specs=[b3((TD, 64)), b3((64, 64))],
        out_shape=[jax.ShapeDtypeStruct((G, TD, 64), F32), jax.ShapeDtypeStruct((G, 64, 64), F32)],
        compiler_params=_params("parallel"),
    )(q, k, kT, v, g, gn, S_prev, dec_g, cross_g, tail_g, gl_g)


def _conv_sample_kernel(ext_ref, w_ref, b_ref, g_ref, bb_ref, o_ref, *, TD):
    acc = jnp.zeros(o_ref.shape, F32)
    for j in range(CONV_K):
        acc = acc + ext_ref[:, j:j + TD, :] * w_ref[j:j + 1, :]
    o_ref[...] = _conv_finish(acc, b_ref[...], g_ref[...], bb_ref[...])


def _conv_sample(ext, w, b, g, bb, TD):
    DB = ext.shape[0]
    vec = lambda a: a.reshape(1, CONV_W)
    return pl.pallas_call(
        functools.partial(_conv_sample_kernel, TD=TD),
        out_shape=jax.ShapeDtypeStruct((DB, TD, CONV_W), F32),
    )(ext, w, vec(b), vec(g), vec(bb))


def _pick(n, pref):
    t = min(n, pref)
    while n % t:
        t //= 2
    return t


def kernel(x_prompt, x_sample, cache_k, cache_v, state_conv, state_ret, page_table, norm_mix, w_in, conv_w, conv_b, conv_ln_g, conv_ln_b, lam_q1, lam_k1, lam_q2, lam_k2, diff_subln, ret_gn, w_out, norm_mlp, w_up, w_down, norm_final):
    B, S, D = x_prompt.shape
    DB, TD, _ = x_sample.shape
    depth = w_in.shape[0]
    n_phys, page = cache_k.shape[1], cache_k.shape[2]
    n_pages = page_table.shape[1]
    past_len = n_pages * page
    Mp, Ms = B * S, DB * TD

    tm_in = _pick(S, 512)
    T_attn = _pick(S, 512)
    L_ret = _pick(S, 256)
    tc_conv = _pick(S, 512)
    tm_mlp = _pick(Mp, 1024)
    tf_mlp = _pick(w_up.shape[2], 1024)
    P = _pick(n_pages, 8)
    Gb = _pick(DB * RET_H, 8)

    w_in_bf = w_in.astype(BF16)
    w_out_bf = w_out.astype(BF16)
    w_up_bf = w_up.astype(BF16)
    w_down_bf = w_down.astype(BF16)

    tab_p = _rope_table(jnp.arange(S))
    tab_s = jnp.tile(_rope_table(past_len + jnp.arange(TD)), (DB, 1))
    lam_all = jnp.stack([lam_q1, lam_k1, lam_q2, lam_k2], axis=1)
    w_vt_bf = jnp.swapaxes(w_in[:, :, _OFF_VD:_OFF_QR], 1, 2).astype(BF16)
    cache_kt3 = jnp.transpose(cache_k, (0, 1, 3, 4, 2)).reshape(
        depth * n_phys, 2 * DIFF_H * DIFF_DQK, page)
    cache_v3 = cache_v.reshape(depth * n_phys, page * DIFF_H, DIFF_DV)
    pt_flat = page_table.reshape(-1)

    xp = x_prompt.reshape(Mp, D)
    xs = x_sample.reshape(Ms, D)
    kp, vp, cp, rp, ks_, vs_, cs_, rs_ = [], [], [], [], [], [], [], []
    for l in range(depth):
        lam_init = 0.8 - 0.6 * math.exp(-0.3 * l)
        last = l == depth - 1

        u, qk, k_d, v_d, vt, ret, g_r = _inproj(xp, norm_mix[l], w_in_bf[l], w_vt_bf[l], tab_p, tm_in)
        o_d = _attn_prompt(qk, vt, lam_all[l], diff_subln[l], lam_init, B, S, T_attn)
        o_r, s_fin = _ret_prompt(ret, g_r, ret_gn[l], B, S, L_ret)
        c = _conv_prompt(u, conv_w[l], conv_b[l], conv_ln_g[l], conv_ln_b[l], B, S, tc_conv)
        xp = _outproj_mlp(xp, c, o_d, o_r, w_out_bf[l], norm_mlp[l], w_up_bf[l], w_down_bf[l],
                          norm_final, last, tm_mlp, tf_mlp)
        kp.append(k_d.reshape(B, S, 2 * DIFF_H, DIFF_DQK))
        vp.append(v_d.reshape(B, S, DIFF_H, DIFF_DV))
        cp.append(u.reshape(B, S, CONV_W)[:, S - (CONV_K - 1):])
        rp.append(s_fin)

        u, qk, k_d, v_d, _, ret, g_r = _inproj(xs, norm_mix[l], w_in_bf[l], w_vt_bf[l], tab_s, Ms)
        q3 = qk[:, 0:512].astype(F32).reshape(DB, TD, 512)
        o_d = _attn_sample(q3, k_d.reshape(DB, TD, 512), v_d.reshape(DB, TD, 512), cache_kt3,
                           cache_v3, pt_flat, lam_all[l], diff_subln[l], lam_init,
                           l, n_phys, n_pages, P)
        to_g = lambda a: a.astype(F32).reshape(DB, TD, RET_H, 64).swapaxes(1, 2).reshape(
            DB * RET_H, TD, 64)
        gn_g = jnp.tile(ret_gn[l].reshape(RET_H, 1, RET_DV), (DB, 1, 1))
        o_r, s_new = _ret_sample(to_g(ret[:, 0:256]), to_g(ret[:, 256:512]), to_g(ret[:, 512:768]),
                                 to_g(g_r), gn_g, state_ret[l].reshape(DB * RET_H, RET_DK, RET_DV),
                                 TD, Gb)
        o_r = o_r.reshape(DB, RET_H, TD, RET_DV).swapaxes(1, 2).reshape(Ms, RET_W)
        ext = jnp.concatenate([state_conv[l], u.reshape(DB, TD, CONV_W)], axis=1)
        c = _conv_sample(ext, conv_w[l], conv_b[l], conv_ln_g[l], conv_ln_b[l], TD)
        xs = _outproj_mlp(xs, c.reshape(Ms, CONV_W), o_d.reshape(Ms, DIFF_W), o_r, w_out_bf[l],
                          norm_mlp[l], w_up_bf[l], w_down_bf[l], norm_final, last, Ms, tf_mlp)
        ks_.append(k_d.reshape(DB, TD, 2 * DIFF_H, DIFF_DQK))
        vs_.append(v_d.reshape(DB, TD, DIFF_H, DIFF_DV))
        cs_.append(ext[:, TD:])
        rs_.append(s_new.reshape(DB, RET_H, RET_DK, RET_DV))

    return (xp.reshape(B, S, D), xs.reshape(DB, TD, D),
            jnp.stack(kp), jnp.stack(vp), jnp.stack(cp), jnp.stack(rp),
            jnp.stack(ks_), jnp.stack(vs_), jnp.stack(cs_), jnp.stack(rs_))
```

```python
import functools
import math

import jax
import jax.numpy as jnp
from jax import lax
from jax.experimental import pallas as pl
from jax.experimental.pallas import tpu as pltpu

F32 = jnp.float32
BF16 = jnp.bfloat16

CONV_W = 256
CONV_K = 31
DIFF_H = 4
DIFF_DQK = 64
DIFF_DV = 128
DIFF_W = DIFF_H * DIFF_DV
RET_H = 4
RET_DK = 64
RET_DV = 64
RET_W = RET_H * RET_DV
ROPE_THETA = 500000.0
ROPE_ROT = DIFF_DQK // 4
RET_THETA = 10000.0
EPS = 1e-5
LANES = 128
CONV_HALO = 32
VMEM_LIMIT = 56 * 1024 * 1024

_OFF_C, _OFF_QD, _OFF_KD, _OFF_VD, _OFF_QR, _OFF_KR, _OFF_VR, _OFF_GR, _D_IN = (
    0, 512, 1024, 1536, 2048, 2304, 2560, 2816, 3072)

_NT = (((1,), (1,)), ((), ()))

Q_SCALE = DIFF_DQK ** -0.5 * math.log2(math.e)


def _params(*sem):
    return pltpu.CompilerParams(dimension_semantics=sem, vmem_limit_bytes=VMEM_LIMIT)


def _rms(x, g):
    return x * lax.rsqrt(jnp.mean(x * x, axis=-1, keepdims=True) + EPS) * g


def _silu(x):
    return x / (1.0 + jnp.exp(-x))


def _lam(lam_ref, lam_init):
    s1 = jnp.sum(lam_ref[0:1, :] * lam_ref[1:2, :], axis=-1, keepdims=True)
    s2 = jnp.sum(lam_ref[2:3, :] * lam_ref[3:4, :], axis=-1, keepdims=True)
    return jnp.exp(s1) - jnp.exp(s2) + lam_init


def _rope_table(pos):
    posf = pos.astype(F32)[:, None]
    n = pos.shape[0]

    def one(n_rot, theta, hd):
        half = n_rot // 2
        inv = 1.0 / (theta ** (jnp.arange(half, dtype=F32) * 2.0 / n_rot))
        ang = posf * inv[None, :]
        cos, sin = jnp.cos(ang), jnp.sin(ang)
        c = jnp.concatenate([cos, cos, jnp.ones((n, hd - n_rot), F32)], axis=-1)
        s = jnp.concatenate([-sin, sin, jnp.zeros((n, hd - n_rot), F32)], axis=-1)
        return jnp.tile(c, (1, LANES // hd)), jnp.tile(s, (1, LANES // hd))

    cd, sd = one(ROPE_ROT, ROPE_THETA, DIFF_DQK)
    cr, sr = one(RET_DK, RET_THETA, RET_DK)
    return jnp.concatenate([cd, sd, cr, sr], axis=-1)


def _log_gamma():
    return jnp.log(1.0 - jnp.exp2(-5.0 - jnp.arange(RET_H, dtype=F32)))


def _ret_tables(L):
    lg = _log_gamma()
    idx = jnp.arange(L, dtype=F32)
    diff = idx[:, None] - idx[None, :]
    decay = jnp.where(diff >= 0, jnp.exp(jnp.maximum(diff, 0.0)[None] * lg[:, None, None]), 0.0)
    cross = jnp.exp((idx[:, None] + 1.0) * lg[None, :])
    tail = jnp.exp((L - 1.0 - idx)[:, None] * lg[None, :])
    gl = jnp.exp(L * lg)
    return decay, cross, tail, gl


def _inproj_kernel(x_ref, g_ref, w_ref, wvt_ref, t_ref, u_ref, qk_ref, k_ref, v_ref, vt_ref,
                   ret_ref, gr_ref):
    h = _rms(x_ref[...], g_ref[...]).astype(BF16)

    def proj(a, b):
        return jnp.dot(h, w_ref[:, a:b], preferred_element_type=F32)

    lane = lax.broadcasted_iota(jnp.int32, (1, LANES), 1) % 64

    def rope(xg, c, s, half):
        partner = jnp.where(lane < half, pltpu.roll(xg, LANES - half, 1), pltpu.roll(xg, half, 1))
        return xg * c + partner * s

    cd, sd = t_ref[:, 0:128], t_ref[:, 128:256]
    cr, sr = t_ref[:, 256:384], t_ref[:, 384:512]

    c_ag = proj(_OFF_C, _OFF_QD)
    u_ref[...] = c_ag[:, :CONV_W] / (1.0 + jnp.exp(-c_ag[:, CONV_W:]))

    q = proj(_OFF_QD, _OFF_KD)
    for j in range(4):
        sl = slice(LANES * j, LANES * (j + 1))
        qk_ref[:, sl] = (rope(q[:, sl], cd, sd, ROPE_ROT // 2) * Q_SCALE).astype(BF16)
    k = proj(_OFF_KD, _OFF_VD)
    for j in range(4):
        sl = slice(LANES * j, LANES * (j + 1))
        kj = rope(k[:, sl], cd, sd, ROPE_ROT // 2)
        k_ref[:, sl] = kj
        qk_ref[:, 512 + LANES * j:512 + LANES * (j + 1)] = kj.astype(BF16)
    v = proj(_OFF_VD, _OFF_QR)
    tm = v.shape[0]
    for j in range(DIFF_H):
        v_ref[pl.ds(j, tm, stride=DIFF_H), :] = v[:, DIFF_DV * j:DIFF_DV * (j + 1)]
    vt_ref[0] = lax.dot_general(wvt_ref[...], h, _NT, preferred_element_type=F32).astype(BF16)

    r = proj(_OFF_QR, _OFF_GR)
    for j in range(2):
        sl = slice(LANES * j, LANES * (j + 1))
        ret_ref[:, sl] = rope(r[:, sl], cr, sr, RET_DK // 2).astype(BF16)
    for j in range(2, 4):
        sl = slice(LANES * j, LANES * (j + 1))
        ret_ref[:, sl] = (rope(r[:, sl], cr, sr, RET_DK // 2) * (RET_DK ** -0.5)).astype(BF16)
    ret_ref[:, 512:768] = r[:, 512:768].astype(BF16)
    gr_ref[...] = proj(_OFF_GR, _D_IN)


def _inproj(x2d, g, w_bf, wvt_bf, table, tm):
    M, D = x2d.shape
    n_t = table.shape[0] // tm
    row = lambda i: (i, 0)
    return pl.pallas_call(
        _inproj_kernel,
        grid=(M // tm,),
        in_specs=[
            pl.BlockSpec((tm, D), row),
            pl.BlockSpec((1, D), lambda i: (0, 0)),
            pl.BlockSpec((D, _D_IN), lambda i: (0, 0)),
            pl.BlockSpec((DIFF_W, D), lambda i: (0, 0)),
            pl.BlockSpec((tm, 512), lambda i: (i % n_t, 0)),
        ],
        out_specs=[
            pl.BlockSpec((tm, CONV_W), row),
            pl.BlockSpec((tm, 1024), row),
            pl.BlockSpec((tm, 512), row),
            pl.BlockSpec((tm * DIFF_H, DIFF_DV), row),
            pl.BlockSpec((1, DIFF_W, tm), lambda i: (i, 0, 0)),
            pl.BlockSpec((tm, 768), row),
            pl.BlockSpec((tm, RET_W), row),
        ],
        out_shape=[
            jax.ShapeDtypeStruct((M, CONV_W), F32),
            jax.ShapeDtypeStruct((M, 1024), BF16),
            jax.ShapeDtypeStruct((M, 512), F32),
            jax.ShapeDtypeStruct((M * DIFF_H, DIFF_DV), F32),
            jax.ShapeDtypeStruct((M // tm, DIFF_W, tm), BF16),
            jax.ShapeDtypeStruct((M, 768), BF16),
            jax.ShapeDtypeStruct((M, RET_W), F32),
        ],
        compiler_params=_params("parallel"),
    )(x2d, g.reshape(1, D), w_bf, wvt_bf, table)


def _attn_prompt_kernel(q_ref, k_ref, vt_ref, lam_ref, sub_ref, o_ref,
                        qs, sa_ref, sb_ref, m_ref, l_ref, acc_ref, *, T, NC, lam_init):
    qi = pl.program_id(2)
    C = T // NC
    q = q_ref[...]
    lane = lax.broadcasted_iota(jnp.int32, (T, LANES), 1)
    qs[0] = jnp.where(lane < DIFF_DQK, q, jnp.zeros_like(q))
    qs[1] = jnp.where(lane >= DIFF_DQK, q, jnp.zeros_like(q))
    m_ref[...] = jnp.full(m_ref.shape, -jnp.inf, F32)
    l_ref[...] = jnp.zeros(l_ref.shape, F32)
    acc_ref[...] = jnp.zeros(acc_ref.shape, F32)

    chains = [(mp, c) for mp in range(2) for c in range(NC)]

    def scores(ki, s_ref):
        k = k_ref[pl.ds(pl.multiple_of(ki * T, T), T), :]
        for mp, c in chains:
            cols = slice(c * C, (c + 1) * C)
            s_ref[mp, :, cols] = lax.dot_general(k, qs[mp, cols, :], _NT,
                                                 preferred_element_type=F32)

    def accumulate(ki, s_ref, diagonal):
        for mp, c in chains:
            cols = slice(c * C, (c + 1) * C)
            nk = (c + 1) * C if diagonal else T
            s = s_ref[mp, 0:nk, cols]
            if diagonal:
                key = lax.broadcasted_iota(jnp.int32, (nk, C), 0)
                qq = lax.broadcasted_iota(jnp.int32, (nk, C), 1) + c * C
                s = jnp.where(key <= qq, s, -jnp.inf)
            m_prev = m_ref[mp, :, cols]
            m_new = jnp.maximum(m_prev, jnp.max(s, axis=0, keepdims=True))
            a = jnp.exp2(m_prev - m_new)
            p = jnp.exp2(s - m_new)
            l_ref[mp, :, cols] = a * l_ref[mp, :, cols] + jnp.sum(p, axis=0, keepdims=True)
            acc_ref[mp, :, cols] = a * acc_ref[mp, :, cols] + jnp.dot(
                vt_ref[ki, :, 0:nk], p.astype(BF16), preferred_element_type=F32)
            m_ref[mp, :, cols] = m_new

    scores(0, sa_ref)

    def body(j, carry):
        ki = 2 * j
        scores(ki + 1, sb_ref)
        accumulate(ki, sa_ref, False)
        scores(ki + 2, sa_ref)
        accumulate(ki + 1, sb_ref, False)
        return carry

    lax.fori_loop(0, qi // 2, body, 0)

    @pl.when(qi % 2 == 0)
    def _():
        accumulate(qi, sa_ref, True)

    @pl.when(qi % 2 == 1)
    def _():
        scores(qi, sb_ref)
        accumulate(qi - 1, sa_ref, False)
        accumulate(qi, sb_ref, True)

    lam = _lam(lam_ref, lam_init)
    d = acc_ref[0] / l_ref[0] - lam * (acc_ref[1] / l_ref[1])
    r = lax.rsqrt(jnp.mean(d * d, axis=0, keepdims=True) + EPS)
    y = d * r * (sub_ref[...] * (1.0 - lam_init))
    o_ref[...] = y.T.astype(BF16)


def _attn_prompt(qk, vt, lam4, subln, lam_init, B, S, T, NC=2):
    M = qk.shape[0]
    nq = S // T
    return pl.pallas_call(
        functools.partial(_attn_prompt_kernel, T=T, NC=NC, lam_init=lam_init),
        grid=(B, DIFF_H, nq),
        in_specs=[
            pl.BlockSpec((T, LANES), lambda b, h, i: (b * nq + i, h)),
            pl.BlockSpec((S, LANES), lambda b, h, i: (b, DIFF_H + h)),
            pl.BlockSpec((nq, DIFF_DV, T), lambda b, h, i: (b, h, 0)),
            pl.BlockSpec((4, DIFF_DQK), lambda b, h, i: (0, 0)),
            pl.BlockSpec((DIFF_DV, 1), lambda b, h, i: (0, 0)),
        ],
        out_specs=pl.BlockSpec((T, LANES), lambda b, h, i: (b * nq + i, h)),
        out_shape=jax.ShapeDtypeStruct((M, DIFF_W), BF16),
        scratch_shapes=[
            pltpu.VMEM((2, T, LANES), BF16),
            pltpu.VMEM((2, T, T), F32),
            pltpu.VMEM((2, T, T), F32),
            pltpu.VMEM((2, 1, T), F32),
            pltpu.VMEM((2, 1, T), F32),
            pltpu.VMEM((2, DIFF_DV, T), F32),
        ],
        compiler_params=_params("parallel", "parallel", "arbitrary"),
    )(qk, qk, vt, lam4, subln.reshape(DIFF_DV, 1))


def _group_norm_gate(o, g, gn):
    hid = lax.broadcasted_iota(jnp.int32, (1, RET_W), 1) // RET_DV

    def group_mean(x):
        out = jnp.zeros_like(x)
        for h in range(RET_H):
            mh = hid == h
            s = jnp.sum(jnp.where(mh, x, 0.0), axis=-1, keepdims=True) * (1.0 / RET_DV)
            out = jnp.where(mh, s, out)
        return out

    d = o - group_mean(o)
    var = group_mean(d * d)
    return d * lax.rsqrt(var + EPS) * gn * _silu(g)


def _ret_prompt_kernel(q_ref, k_ref, v_ref, g_ref, gn_ref, dec_ref, cross_ref, tail_ref, gl_ref,
                       o_ref, s_out_ref, S_ref):
    c = pl.program_id(1)

    @pl.when(c == 0)
    def _():
        S_ref[...] = jnp.zeros(S_ref.shape, F32)

    q = q_ref[...]
    k = k_ref[...]
    v = v_ref[...]
    hid = lax.broadcasted_iota(jnp.int32, (1, RET_W), 1) // RET_DV
    Sb = S_ref[...]
    o = jnp.dot(q, Sb.astype(BF16), preferred_element_type=F32) * cross_ref[...]
    for h in range(RET_H):
        mh = hid == h
        qh = jnp.where(mh, q, jnp.zeros_like(q))
        sc = lax.dot_general(qh, k, _NT, preferred_element_type=F32) * dec_ref[h]
        oh = jnp.dot(sc.astype(BF16), v, preferred_element_type=F32)
        o = o + jnp.where(mh, oh, 0.0)
    kt = (k.astype(F32) * tail_ref[...]).astype(BF16)
    upd = lax.dot_general(kt, v, (((0,), (0,)), ((), ())), preferred_element_type=F32)
    rowh = lax.broadcasted_iota(jnp.int32, (RET_W, RET_W), 0) // RET_DK
    colh = lax.broadcasted_iota(jnp.int32, (RET_W, RET_W), 1) // RET_DV
    S_new = gl_ref[...] * Sb + jnp.where(rowh == colh, upd, 0.0)
    S_ref[...] = S_new
    o_ref[...] = _group_norm_gate(o, g_ref[...], gn_ref[...]).astype(BF16)

    @pl.when(c == pl.num_programs(1) - 1)
    def _():
        y = S_new
        for sft in (64, 128, 192):
            y = y + pltpu.roll(S_new, sft, 1)
        s_out_ref[0] = y[:, 0:RET_DV]


def _ret_prompt(ret, gr, gn, B, S, L):
    M = ret.shape[0]
    nc = S // L
    decay, cross, tail, gl = _ret_tables(L)
    rep = lambda a: jnp.repeat(a, RET_DV, axis=-1)
    row = lambda b, c: (b * nc + c, 0)
    const2 = lambda b, c: (0, 0)
    o_r, s_fin = pl.pallas_call(
        _ret_prompt_kernel,
        grid=(B, nc),
        in_specs=[
            pl.BlockSpec((L, RET_W), lambda b, c: (b * nc + c, 0)),
            pl.BlockSpec((L, RET_W), lambda b, c: (b * nc + c, 1)),
            pl.BlockSpec((L, RET_W), lambda b, c: (b * nc + c, 2)),
            pl.BlockSpec((L, RET_W), row),
            pl.BlockSpec((1, RET_W), const2),
            pl.BlockSpec((RET_H, L, L), lambda b, c: (0, 0, 0)),
            pl.BlockSpec((L, RET_W), const2),
            pl.BlockSpec((L, RET_W), const2),
            pl.BlockSpec((1, RET_W), const2),
        ],
        out_specs=[
            pl.BlockSpec((L, RET_W), row),
            pl.BlockSpec((1, RET_W, RET_DV), lambda b, c: (b, 0, 0)),
        ],
        out_shape=[
            jax.ShapeDtypeStruct((M, RET_W), BF16),
            jax.ShapeDtypeStruct((B, RET_W, RET_DV), F32),
        ],
        scratch_shapes=[pltpu.VMEM((RET_W, RET_W), F32)],
        compiler_params=_params("parallel", "arbitrary"),
    )(ret, ret, ret, gr, gn.reshape(1, RET_W), decay, rep(cross), rep(tail), rep(gl[None, :]))
    return o_r, s_fin.reshape(B, RET_H, RET_DK, RET_DV)


def _conv_finish(acc, b, g, bb):
    y = acc + b
    mu = jnp.mean(y, axis=-1, keepdims=True)
    d = y - mu
    var = jnp.mean(d * d, axis=-1, keepdims=True)
    return _silu(d * lax.rsqrt(var + EPS) * g + bb)


def _conv_prompt_kernel(prev_ref, cur_ref, w_ref, b_ref, g_ref, bb_ref, o_ref, buf, sh, *, tc, sub):
    i = pl.program_id(1)
    prev = prev_ref[...]
    buf[0:CONV_HALO, :] = jnp.where(i == 0, jnp.zeros_like(prev), prev)
    buf[CONV_HALO:CONV_HALO + tc, :] = cur_ref[...]
    off = CONV_HALO - (CONV_K - 1)
    cls = [[j for j in range(CONV_K) if (off + j) % 8 == a] for a in range(8)]
    for a, taps in enumerate(cls):
        n = tc + taps[-1] - taps[0]
        sh[a, 0:n, :] = buf[off + taps[0]:off + taps[0] + n, :]

    for r0 in range(0, tc, sub):
        acc = jnp.zeros((sub, CONV_W), F32)
        for a, taps in enumerate(cls):
            for j in taps:
                acc = acc + sh[a, r0 + j - taps[0]:r0 + j - taps[0] + sub, :] * w_ref[j:j + 1, :]
        o_ref[r0:r0 + sub, :] = _conv_finish(acc, b_ref[...], g_ref[...], bb_ref[...]).astype(BF16)


def _conv_prompt(u, w, b, g, bb, B, S, tc, sub=64):
    M = u.shape[0]
    nb = S // tc
    hb = tc // CONV_HALO
    vec = lambda a: a.reshape(1, CONV_W)
    const2 = lambda bi, i: (0, 0)
    return pl.pallas_call(
        functools.partial(_conv_prompt_kernel, tc=tc, sub=sub),
        grid=(B, nb),
        in_specs=[
            pl.BlockSpec((CONV_HALO, CONV_W),
                         lambda bi, i: (jnp.maximum((bi * nb + i) * hb - 1, 0), 0)),
            pl.BlockSpec((tc, CONV_W), lambda bi, i: (bi * nb + i, 0)),
            pl.BlockSpec((CONV_HALO, CONV_W), const2),
            pl.BlockSpec((1, CONV_W), const2),
            pl.BlockSpec((1, CONV_W), const2),
            pl.BlockSpec((1, CONV_W), const2),
        ],
        out_specs=pl.BlockSpec((tc, CONV_W), lambda bi, i: (bi * nb + i, 0)),
        out_shape=jax.ShapeDtypeStruct((M, CONV_W), BF16),
        scratch_shapes=[pltpu.VMEM((CONV_HALO + tc, CONV_W), F32),
                        pltpu.VMEM((8, tc + CONV_HALO - 8, CONV_W), F32)],
        compiler_params=_params("parallel", "arbitrary"),
    )(u, u, jnp.pad(w, ((0, CONV_HALO - CONV_K), (0, 0))), vec(b), vec(g), vec(bb))


def _outproj_mlp_kernel(x_ref, c_ref, od_ref, or_ref, wo_ref, g_ref, wu_ref, wd_ref, gf_ref,
                        o_ref, acc_ref, h2_ref, *, final_norm):
    f = pl.program_id(1)

    @pl.when(f == 0)
    def _():
        mix = jnp.dot(c_ref[...].astype(BF16), wo_ref[0:CONV_W, :], preferred_element_type=F32)
        mix += jnp.dot(od_ref[...].astype(BF16), wo_ref[CONV_W:CONV_W + DIFF_W, :],
                       preferred_element_type=F32)
        mix += jnp.dot(or_ref[...].astype(BF16), wo_ref[CONV_W + DIFF_W:, :],
                       preferred_element_type=F32)
        xm = x_ref[...] + mix
        acc_ref[...] = xm
        h2_ref[...] = _rms(xm, g_ref[...]).astype(BF16)

    up = jnp.dot(h2_ref[...], wu_ref[...], preferred_element_type=F32)
    a = jnp.square(jnp.maximum(up, 0.0)).astype(BF16)
    acc_ref[...] += jnp.dot(a, wd_ref[...], preferred_element_type=F32)

    @pl.when(f == pl.num_programs(1) - 1)
    def _():
        y = acc_ref[...]
        if final_norm:
            y = _rms(y, gf_ref[...])
        o_ref[...] = y


def _outproj_mlp(x2d, c, od, orr, wo_bf, g, wu_bf, wd_bf, gf, final_norm, tm, tf):
    M, D = x2d.shape
    FF = wu_bf.shape[1]
    row = lambda i, f: (i, 0)
    const2 = lambda i, f: (0, 0)
    return pl.pallas_call(
        functools.partial(_outproj_mlp_kernel, final_norm=final_norm),
        grid=(M // tm, FF // tf),
        in_specs=[
            pl.BlockSpec((tm, D), row),
            pl.BlockSpec((tm, CONV_W), row),
            pl.BlockSpec((tm, DIFF_W), row),
            pl.BlockSpec((tm, RET_W), row),
            pl.BlockSpec((D, D), const2),
            pl.BlockSpec((1, D), const2),
            pl.BlockSpec((D, tf), lambda i, f: (0, f)),
            pl.BlockSpec((tf, D), lambda i, f: (f, 0)),
            pl.BlockSpec((1, D), const2),
        ],
        out_specs=pl.BlockSpec((tm, D), row),
        out_shape=jax.ShapeDtypeStruct((M, D), F32),
        scratch_shapes=[pltpu.VMEM((tm, D), F32), pltpu.VMEM((tm, D), BF16)],
        compiler_params=_params("parallel", "arbitrary"),
    )(x2d, c, od, orr, wo_bf, g.reshape(1, D), wu_bf, wd_bf, gf.reshape(1, D))


def _attn_sample_kernel(pt_ref, q_ref, kn_ref, vn_ref, lam_ref, sub_ref, *rest,
                        P, TD, page, lam_init):
    kp = rest[0:P]
    vp = rest[P:2 * P]
    o_ref, wq, m_ref, l_ref, acc_ref = rest[2 * P:]
    R = 8 * TD
    s_id = pl.program_id(1)

    @pl.when(s_id == 0)
    def _():
        q = q_ref[0]
        hm = lax.broadcasted_iota(jnp.int32, (8, 512), 0)
        lh = lax.broadcasted_iota(jnp.int32, (8, 512), 1) // DIFF_DQK
        for t in range(TD):
            wq[8 * t:8 * t + 8, :] = jnp.where(hm == lh, jnp.broadcast_to(q[t:t + 1, :], (8, 512)), 0.0)
        m_ref[...] = jnp.full(m_ref.shape, -jnp.inf, F32)
        l_ref[...] = jnp.zeros(l_ref.shape, F32)
        acc_ref[...] = jnp.zeros(acc_ref.shape, F32)

    def update(s, vmat):
        m_prev = m_ref[...]
        m_new = jnp.maximum(m_prev, jnp.max(s, axis=-1, keepdims=True))
        a = jnp.exp2(m_prev - m_new)
        p = jnp.exp2(s - m_new)
        l_ref[...] = a * l_ref[...] + jnp.sum(p, axis=-1, keepdims=True)
        acc_ref[...] = a * acc_ref[...] + jnp.dot(p.astype(BF16), vmat, preferred_element_type=F32)
        m_ref[...] = m_new

    wqb = wq[...].astype(BF16)
    kcat = jnp.concatenate([r[0].astype(BF16) for r in kp], axis=1)

    def v_page(r):
        heads = [r[0, pl.ds(h, page, stride=DIFF_H), :] for h in range(DIFF_H)]
        return jnp.concatenate(heads, axis=1).astype(BF16)

    vcat = jnp.concatenate([v_page(r) for r in vp], axis=0)
    update(jnp.dot(wqb, kcat, preferred_element_type=F32), vcat)

    @pl.when(s_id == pl.num_programs(1) - 1)
    def _():
        pad = jnp.zeros((16 - TD, 512), F32)
        kn = jnp.concatenate([kn_ref[0], pad], axis=0).astype(BF16)
        vn = jnp.concatenate([vn_ref[0], pad], axis=0).astype(BF16)
        s2 = lax.dot_general(wqb, kn, _NT, preferred_element_type=F32)
        tq = lax.broadcasted_iota(jnp.int32, (R, 16), 0) // 8
        tk = lax.broadcasted_iota(jnp.int32, (R, 16), 1)
        update(jnp.where(tk <= tq, s2, -jnp.inf), vn)

        lam = _lam(lam_ref, lam_init)
        o = acc_ref[...] / l_ref[...]
        hm = lax.broadcasted_iota(jnp.int32, (8, 512), 0)
        lh = lax.broadcasted_iota(jnp.int32, (8, 512), 1) // DIFF_DV
        coef = jnp.where(hm // 2 == lh, jnp.where(hm % 2 == 0, 1.0, -lam), 0.0)
        for t in range(TD):
            d = jnp.sum(o[8 * t:8 * t + 8, :] * coef, axis=0, keepdims=True)
            for h in range(DIFF_H):
                sl = slice(DIFF_DV * h, DIFF_DV * (h + 1))
                o_ref[0, t:t + 1, sl] = _rms(d[:, sl], sub_ref[...]) * (1.0 - lam_init)


def _attn_sample(q3, kn3, vn3, cache_kt3, cache_v3, pt_flat, lam4, subln, lam_init, layer, n_phys,
                 n_pages, P):
    DB, TD, _ = q3.shape
    page = cache_kt3.shape[2]
    base = layer * n_phys

    def page_map(j):
        return lambda b, s, pt: (base + pt[b * n_pages + s * P + j], 0, 0)

    tok = pl.BlockSpec((1, TD, 512), lambda b, s, pt: (b, 0, 0))
    in_specs = [
        tok, tok, tok,
        pl.BlockSpec((4, DIFF_DQK), lambda b, s, pt: (0, 0)),
        pl.BlockSpec((1, DIFF_DV), lambda b, s, pt: (0, 0)),
    ]
    in_specs += [pl.BlockSpec((1, 512, page), page_map(j)) for j in range(P)]
    in_specs += [pl.BlockSpec((1, page * DIFF_H, DIFF_DV), page_map(j)) for j in range(P)]
    R = 8 * TD
    grid_spec = pltpu.PrefetchScalarGridSpec(
        num_scalar_prefetch=1,
        grid=(DB, n_pages // P),
        in_specs=in_specs,
        out_specs=pl.BlockSpec((1, TD, 512), lambda b, s, pt: (b, 0, 0)),
        scratch_shapes=[
            pltpu.VMEM((R, 512), F32),
            pltpu.VMEM((R, 1), F32),
            pltpu.VMEM((R, 1), F32),
            pltpu.VMEM((R, 512), F32),
        ],
    )
    return pl.pallas_call(
        functools.partial(_attn_sample_kernel, P=P, TD=TD, page=page, lam_init=lam_init),
        grid_spec=grid_spec,
        out_shape=jax.ShapeDtypeStruct((DB, TD, 512), F32),
        compiler_params=_params("parallel", "arbitrary"),
    )(pt_flat, q3, kn3, vn3, lam4, subln.reshape(1, DIFF_DV),
      *([cache_kt3] * P), *([cache_v3] * P))


def _ret_sample_kernel(q_ref, k_ref, kT_ref, v_ref, g_ref, gn_ref, S_ref, dec_ref, cross_ref,
                       tail_ref, gl_ref, o_ref, s_out_ref, *, TD):
    q = q_ref[...]
    k = k_ref[...]
    v = v_ref[...]
    S = S_ref[...]
    sc = jnp.einsum('gtd,gsd->gts', q, k, preferred_element_type=F32) * dec_ref[...]
    o = jnp.einsum('gtd,gde->gte', q, S, preferred_element_type=F32) * cross_ref[...]
    for s in range(TD):
        o = o + sc[:, :, s:s + 1] * v[:, s:s + 1, :]
    kT = kT_ref[...]
    vt = v * tail_ref[...]
    S_new = gl_ref[...] * S
    for s in range(TD):
        S_new = S_new + kT[:, :, s:s + 1] * vt[:, s:s + 1, :]
    s_out_ref[...] = S_new
    mu = jnp.mean(o, axis=-1, keepdims=True)
    d = o - mu
    var = jnp.mean(d * d, axis=-1, keepdims=True)
    o_ref[...] = d * lax.rsqrt(var + EPS) * gn_ref[...] * _silu(g_ref[...])


def _ret_sample(q, k, v, g, gn, S_prev, TD, Gb):
    G = q.shape[0]
    lg = jnp.tile(_log_gamma(), G // RET_H)
    decay, cross, tail, gl = _ret_tables(TD)
    dec_g = jnp.tile(decay, (G // RET_H, 1, 1))
    cross_g = jnp.tile(cross.T, (G // RET_H, 1))[:, :, None]
    tail_g = jnp.tile(tail.T, (G // RET_H, 1))[:, :, None]
    gl_g = jnp.tile(gl, G // RET_H)[:, None, None]
    del lg
    kT = jnp.swapaxes(k, 1, 2)
    b3 = lambda shp: pl.BlockSpec((Gb,) + shp, lambda i: (i, 0, 0))
    return pl.pallas_call(
        functools.partial(_ret_sample_kernel, TD=TD),
        grid=(G // Gb,),
        in_specs=[b3((TD, 64)), b3((TD, 64)), b3((64, TD)), b3((TD, 64)), b3((TD, 64)), b3((1, 64)),
                  b3((64, 64)), b3((TD, TD)), b3((TD, 1)), b3((TD, 1)), b3((1, 1))],
        out_specs=[b3((TD, 64)), b3((64, 64))],
        out_shape=[jax.ShapeDtypeStruct((G, TD, 64), F32), jax.ShapeDtypeStruct((G, 64, 64), F32)],
        compiler_params=_params("parallel"),
    )(q, k, kT, v, g, gn, S_prev, dec_g, cross_g, tail_g, gl_g)


def _conv_sample_kernel(ext_ref, w_ref, b_ref, g_ref, bb_ref, o_ref, *, TD):
    acc = jnp.zeros(o_ref.shape, F32)
    for j in range(CONV_K):
        acc = acc + ext_ref[:, j:j + TD, :] * w_ref[j:j + 1, :]
    o_ref[...] = _conv_finish(acc, b_ref[...], g_ref[...], bb_ref[...])


def _conv_sample(ext, w, b, g, bb, TD):
    DB = ext.shape[0]
    vec = lambda a: a.reshape(1, CONV_W)
    return pl.pallas_call(
        functools.partial(_conv_sample_kernel, TD=TD),
        out_shape=jax.ShapeDtypeStruct((DB, TD, CONV_W), F32),
    )(ext, w, vec(b), vec(g), vec(bb))


def _pick(n, pref):
    t = min(n, pref)
    while n % t:
        t //= 2
    return t


def kernel(x_prompt, x_sample, cache_k, cache_v, state_conv, state_ret, page_table, norm_mix, w_in, conv_w, conv_b, conv_ln_g, conv_ln_b, lam_q1, lam_k1, lam_q2, lam_k2, diff_subln, ret_gn, w_out, norm_mlp, w_up, w_down, norm_final):
    B, S, D = x_prompt.shape
    DB, TD, _ = x_sample.shape
    depth = w_in.shape[0]
    n_phys, page = cache_k.shape[1], cache_k.shape[2]
    n_pages = page_table.shape[1]
    past_len = n_pages * page
    Mp, Ms = B * S, DB * TD

    T_attn = _pick(S, 512)
    tm_in = T_attn
    L_ret = _pick(S, 256)
    tc_conv = _pick(S, 512)
    tm_mlp = _pick(Mp, 1024)
    tf_mlp = _pick(w_up.shape[2], 1024)
    P = _pick(n_pages, 8)
    Gb = _pick(DB * RET_H, 8)

    w_in_bf = w_in.astype(BF16)
    w_out_bf = w_out.astype(BF16)
    w_up_bf = w_up.astype(BF16)
    w_down_bf = w_down.astype(BF16)

    tab_p = _rope_table(jnp.arange(S))
    tab_s = jnp.tile(_rope_table(past_len + jnp.arange(TD)), (DB, 1))
    lam_all = jnp.stack([lam_q1, lam_k1, lam_q2, lam_k2], axis=1)
    w_vt_bf = jnp.swapaxes(w_in[:, :, _OFF_VD:_OFF_QR], 1, 2).astype(BF16)
    cache_kt3 = jnp.transpose(cache_k, (0, 1, 3, 4, 2)).reshape(
        depth * n_phys, 2 * DIFF_H * DIFF_DQK, page)
    cache_v3 = cache_v.reshape(depth * n_phys, page * DIFF_H, DIFF_DV)
    pt_flat = page_table.reshape(-1)

    xp = x_prompt.reshape(Mp, D)
    xs = x_sample.reshape(Ms, D)
    kp, vp, cp, rp, ks_, vs_, cs_, rs_ = [], [], [], [], [], [], [], []
    for l in range(depth):
        lam_init = 0.8 - 0.6 * math.exp(-0.3 * l)
        last = l == depth - 1

        u, qk, k_d, v_d, vt, ret, g_r = _inproj(xp, norm_mix[l], w_in_bf[l], w_vt_bf[l], tab_p, tm_in)
        o_d = _attn_prompt(qk, vt, lam_all[l], diff_subln[l], lam_init, B, S, T_attn)
        o_r, s_fin = _ret_prompt(ret, g_r, ret_gn[l], B, S, L_ret)
        c = _conv_prompt(u, conv_w[l], conv_b[l], conv_ln_g[l], conv_ln_b[l], B, S, tc_conv)
        xp = _outproj_mlp(xp, c, o_d, o_r, w_out_bf[l], norm_mlp[l], w_up_bf[l], w_down_bf[l],
                          norm_final, last, tm_mlp, tf_mlp)
        kp.append(k_d.reshape(B, S, 2 * DIFF_H, DIFF_DQK))
        vp.append(v_d.reshape(B, S, DIFF_H, DIFF_DV))
        cp.append(u.reshape(B, S, CONV_W)[:, S - (CONV_K - 1):])
        rp.append(s_fin)

        u, qk, k_d, v_d, _, ret, g_r = _inproj(xs, norm_mix[l], w_in_bf[l], w_vt_bf[l], tab_s, Ms)
        q3 = qk[:, 0:512].astype(F32).reshape(DB, TD, 512)
        o_d = _attn_sample(q3, k_d.reshape(DB, TD, 512), v_d.reshape(DB, TD, 512), cache_kt3,
                           cache_v3, pt_flat, lam_all[l], diff_subln[l], lam_init,
                           l, n_phys, n_pages, P)
        to_g = lambda a: a.astype(F32).reshape(DB, TD, RET_H, 64).swapaxes(1, 2).reshape(
            DB * RET_H, TD, 64)
        gn_g = jnp.tile(ret_gn[l].reshape(RET_H, 1, RET_DV), (DB, 1, 1))
        o_r, s_new = _ret_sample(to_g(ret[:, 0:256]), to_g(ret[:, 256:512]), to_g(ret[:, 512:768]),
                                 to_g(g_r), gn_g, state_ret[l].reshape(DB * RET_H, RET_DK, RET_DV),
                                 TD, Gb)
        o_r = o_r.reshape(DB, RET_H, TD, RET_DV).swapaxes(1, 2).reshape(Ms, RET_W)
        ext = jnp.concatenate([state_conv[l], u.reshape(DB, TD, CONV_W)], axis=1)
        c = _conv_sample(ext, conv_w[l], conv_b[l], conv_ln_g[l], conv_ln_b[l], TD)
        xs = _outproj_mlp(xs, c.reshape(Ms, CONV_W), o_d.reshape(Ms, DIFF_W), o_r, w_out_bf[l],
                          norm_mlp[l], w_up_bf[l], w_down_bf[l], norm_final, last, Ms, tf_mlp)
        ks_.append(k_d.reshape(DB, TD, 2 * DIFF_H, DIFF_DQK))
        vs_.append(v_d.reshape(DB, TD, DIFF_H, DIFF_DV))
        cs_.append(ext[:, TD:])
        rs_.append(s_new.reshape(DB, RET_H, RET_DK, RET_DV))

    return (xp.reshape(B, S, D), xs.reshape(DB, TD, D),
            jnp.stack(kp), jnp.stack(vp), jnp.stack(cp), jnp.stack(rp),
            jnp.stack(ks_), jnp.stack(vs_), jnp.stack(cs_), jnp.stack(rs_))
```

```python
import functools
import math

import jax
import jax.numpy as jnp
from jax import lax
from jax.experimental import pallas as pl
from jax.experimental.pallas import tpu as pltpu

F32 = jnp.float32
BF16 = jnp.bfloat16

CONV_W = 256
CONV_K = 31
DIFF_H = 4
DIFF_DQK = 64
DIFF_DV = 128
DIFF_W = DIFF_H * DIFF_DV
RET_H = 4
RET_DK = 64
RET_DV = 64
RET_W = RET_H * RET_DV
ROPE_THETA = 500000.0
ROPE_ROT = DIFF_DQK // 4
RET_THETA = 10000.0
EPS = 1e-5
LANES = 128
CONV_HALO = 32
VMEM_LIMIT = 56 * 1024 * 1024

_OFF_C, _OFF_QD, _OFF_KD, _OFF_VD, _OFF_QR, _OFF_KR, _OFF_VR, _OFF_GR, _D_IN = (
    0, 512, 1024, 1536, 2048, 2304, 2560, 2816, 3072)

_NT = (((1,), (1,)), ((), ()))

Q_SCALE = DIFF_DQK ** -0.5 * math.log2(math.e)


def _params(*sem):
    return pltpu.CompilerParams(dimension_semantics=sem, vmem_limit_bytes=VMEM_LIMIT)


def _rms(x, g):
    return x * lax.rsqrt(jnp.mean(x * x, axis=-1, keepdims=True) + EPS) * g


def _silu(x):
    return x / (1.0 + jnp.exp(-x))


def _lam(lam_ref, lam_init):
    s1 = jnp.sum(lam_ref[0:1, :] * lam_ref[1:2, :], axis=-1, keepdims=True)
    s2 = jnp.sum(lam_ref[2:3, :] * lam_ref[3:4, :], axis=-1, keepdims=True)
    return jnp.exp(s1) - jnp.exp(s2) + lam_init


def _rope_table(pos):
    posf = pos.astype(F32)[:, None]
    n = pos.shape[0]

    def one(n_rot, theta, hd):
        half = n_rot // 2
        inv = 1.0 / (theta ** (jnp.arange(half, dtype=F32) * 2.0 / n_rot))
        ang = posf * inv[None, :]
        cos, sin = jnp.cos(ang), jnp.sin(ang)
        c = jnp.concatenate([cos, cos, jnp.ones((n, hd - n_rot), F32)], axis=-1)
        s = jnp.concatenate([-sin, sin, jnp.zeros((n, hd - n_rot), F32)], axis=-1)
        return jnp.tile(c, (1, LANES // hd)), jnp.tile(s, (1, LANES // hd))

    cd, sd = one(ROPE_ROT, ROPE_THETA, DIFF_DQK)
    cr, sr = one(RET_DK, RET_THETA, RET_DK)
    return jnp.concatenate([cd, sd, cr, sr], axis=-1)


def _log_gamma():
    return jnp.log(1.0 - jnp.exp2(-5.0 - jnp.arange(RET_H, dtype=F32)))


def _ret_tables(L):
    lg = _log_gamma()
    idx = jnp.arange(L, dtype=F32)
    diff = idx[:, None] - idx[None, :]
    decay = jnp.where(diff >= 0, jnp.exp(jnp.maximum(diff, 0.0)[None] * lg[:, None, None]), 0.0)
    cross = jnp.exp((idx[:, None] + 1.0) * lg[None, :])
    tail = jnp.exp((L - 1.0 - idx)[:, None] * lg[None, :])
    gl = jnp.exp(L * lg)
    return decay, cross, tail, gl


def _inproj_kernel(x_ref, g_ref, w_ref, wvt_ref, t_ref, u_ref, qk_ref, k_ref, v_ref, vt_ref,
                   ret_ref, gr_ref):
    h = _rms(x_ref[...], g_ref[...]).astype(BF16)

    def proj(a, b):
        return jnp.dot(h, w_ref[:, a:b], preferred_element_type=F32)

    lane = lax.broadcasted_iota(jnp.int32, (1, LANES), 1) % 64

    def rope(xg, c, s, half):
        partner = jnp.where(lane < half, pltpu.roll(xg, LANES - half, 1), pltpu.roll(xg, half, 1))
        return xg * c + partner * s

    cd, sd = t_ref[:, 0:128], t_ref[:, 128:256]
    cr, sr = t_ref[:, 256:384], t_ref[:, 384:512]

    c_ag = proj(_OFF_C, _OFF_QD)
    u_ref[...] = c_ag[:, :CONV_W] / (1.0 + jnp.exp(-c_ag[:, CONV_W:]))

    q = proj(_OFF_QD, _OFF_KD)
    for j in range(4):
        sl = slice(LANES * j, LANES * (j + 1))
        qk_ref[:, sl] = (rope(q[:, sl], cd, sd, ROPE_ROT // 2) * Q_SCALE).astype(BF16)
    k = proj(_OFF_KD, _OFF_VD)
    for j in range(4):
        sl = slice(LANES * j, LANES * (j + 1))
        kj = rope(k[:, sl], cd, sd, ROPE_ROT // 2)
        k_ref[:, sl] = kj
        qk_ref[:, 512 + LANES * j:512 + LANES * (j + 1)] = kj.astype(BF16)
    v = proj(_OFF_VD, _OFF_QR)
    tm = v.shape[0]
    for j in range(DIFF_H):
        v_ref[pl.ds(j, tm, stride=DIFF_H), :] = v[:, DIFF_DV * j:DIFF_DV * (j + 1)]
    vt_ref[0] = lax.dot_general(wvt_ref[...], h, _NT, preferred_element_type=F32).astype(BF16)

    r = proj(_OFF_QR, _OFF_GR)
    for j in range(2):
        sl = slice(LANES * j, LANES * (j + 1))
        ret_ref[:, sl] = rope(r[:, sl], cr, sr, RET_DK // 2).astype(BF16)
    for j in range(2, 4):
        sl = slice(LANES * j, LANES * (j + 1))
        ret_ref[:, sl] = (rope(r[:, sl], cr, sr, RET_DK // 2) * (RET_DK ** -0.5)).astype(BF16)
    ret_ref[:, 512:768] = r[:, 512:768].astype(BF16)
    gr_ref[...] = proj(_OFF_GR, _D_IN)


def _inproj(x2d, g, w_bf, wvt_bf, table, tm):
    M, D = x2d.shape
    n_t = table.shape[0] // tm
    row = lambda i: (i, 0)
    return pl.pallas_call(
        _inproj_kernel,
        grid=(M // tm,),
        in_specs=[
            pl.BlockSpec((tm, D), row),
            pl.BlockSpec((1, D), lambda i: (0, 0)),
            pl.BlockSpec((D, _D_IN), lambda i: (0, 0)),
            pl.BlockSpec((DIFF_W, D), lambda i: (0, 0)),
            pl.BlockSpec((tm, 512), lambda i: (i % n_t, 0)),
        ],
        out_specs=[
            pl.BlockSpec((tm, CONV_W), row),
            pl.BlockSpec((tm, 1024), row),
            pl.BlockSpec((tm, 512), row),
            pl.BlockSpec((tm * DIFF_H, DIFF_DV), row),
            pl.BlockSpec((1, DIFF_W, tm), lambda i: (i, 0, 0)),
            pl.BlockSpec((tm, 768), row),
            pl.BlockSpec((tm, RET_W), row),
        ],
        out_shape=[
            jax.ShapeDtypeStruct((M, CONV_W), F32),
            jax.ShapeDtypeStruct((M, 1024), BF16),
            jax.ShapeDtypeStruct((M, 512), F32),
            jax.ShapeDtypeStruct((M * DIFF_H, DIFF_DV), F32),
            jax.ShapeDtypeStruct((M // tm, DIFF_W, tm), BF16),
            jax.ShapeDtypeStruct((M, 768), BF16),
            jax.ShapeDtypeStruct((M, RET_W), F32),
        ],
        compiler_params=_params("parallel"),
    )(x2d, g.reshape(1, D), w_bf, wvt_bf, table)


def _attn_prompt_kernel(q_ref, k_ref, vt_ref, lam_ref, sub_ref, o_ref,
                        qs, sa_ref, sb_ref, m_ref, l_ref, acc_ref, *, T, NC, lam_init):
    qi = pl.program_id(2)
    C = T // NC
    q = q_ref[...]
    lane = lax.broadcasted_iota(jnp.int32, (T, LANES), 1)
    qs[0] = jnp.where(lane < DIFF_DQK, q, jnp.zeros_like(q))
    qs[1] = jnp.where(lane >= DIFF_DQK, q, jnp.zeros_like(q))
    m_ref[...] = jnp.full(m_ref.shape, -jnp.inf, F32)
    l_ref[...] = jnp.zeros(l_ref.shape, F32)
    acc_ref[...] = jnp.zeros(acc_ref.shape, F32)

    chains = [(mp, c) for mp in range(2) for c in range(NC)]

    def scores(ki, s_ref):
        k = k_ref[pl.ds(pl.multiple_of(ki * T, T), T), :]
        for mp, c in chains:
            cols = slice(c * C, (c + 1) * C)
            s_ref[mp, :, cols] = lax.dot_general(k, qs[mp, cols, :], _NT,
                                                 preferred_element_type=F32)

    def accumulate(ki, s_ref, diagonal):
        for mp, c in chains:
            cols = slice(c * C, (c + 1) * C)
            nk = (c + 1) * C if diagonal else T
            s = s_ref[mp, 0:nk, cols]
            if diagonal:
                key = lax.broadcasted_iota(jnp.int32, (nk, C), 0)
                qq = lax.broadcasted_iota(jnp.int32, (nk, C), 1) + c * C
                s = jnp.where(key <= qq, s, -jnp.inf)
            m_prev = m_ref[mp, :, cols]
            m_new = jnp.maximum(m_prev, jnp.max(s, axis=0, keepdims=True))
            a = jnp.exp2(m_prev - m_new)
            p = jnp.exp2(s - m_new)
            l_ref[mp, :, cols] = a * l_ref[mp, :, cols] + jnp.sum(p, axis=0, keepdims=True)
            acc_ref[mp, :, cols] = a * acc_ref[mp, :, cols] + jnp.dot(
                vt_ref[ki, :, 0:nk], p.astype(BF16), preferred_element_type=F32)
            m_ref[mp, :, cols] = m_new

    scores(0, sa_ref)

    def body(j, carry):
        ki = 2 * j
        scores(ki + 1, sb_ref)
        accumulate(ki, sa_ref, False)
        scores(ki + 2, sa_ref)
        accumulate(ki + 1, sb_ref, False)
        return carry

    lax.fori_loop(0, qi // 2, body, 0)

    @pl.when(qi % 2 == 0)
    def _():
        accumulate(qi, sa_ref, True)

    @pl.when(qi % 2 == 1)
    def _():
        scores(qi, sb_ref)
        accumulate(qi - 1, sa_ref, False)
        accumulate(qi, sb_ref, True)

    lam = _lam(lam_ref, lam_init)
    d = acc_ref[0] / l_ref[0] - lam * (acc_ref[1] / l_ref[1])
    r = lax.rsqrt(jnp.mean(d * d, axis=0, keepdims=True) + EPS)
    y = d * r * (sub_ref[...] * (1.0 - lam_init))
    o_ref[...] = y.T.astype(BF16)


def _attn_prompt(qk, vt, lam4, subln, lam_init, B, S, T, NC=2):
    M = qk.shape[0]
    nq = S // T
    return pl.pallas_call(
        functools.partial(_attn_prompt_kernel, T=T, NC=NC, lam_init=lam_init),
        grid=(B, DIFF_H, nq),
        in_specs=[
            pl.BlockSpec((T, LANES), lambda b, h, i: (b * nq + i, h)),
            pl.BlockSpec((S, LANES), lambda b, h, i: (b, DIFF_H + h)),
            pl.BlockSpec((nq, DIFF_DV, T), lambda b, h, i: (b, h, 0)),
            pl.BlockSpec((4, DIFF_DQK), lambda b, h, i: (0, 0)),
            pl.BlockSpec((DIFF_DV, 1), lambda b, h, i: (0, 0)),
        ],
        out_specs=pl.BlockSpec((T, LANES), lambda b, h, i: (b * nq + i, h)),
        out_shape=jax.ShapeDtypeStruct((M, DIFF_W), BF16),
        scratch_shapes=[
            pltpu.VMEM((2, T, LANES), BF16),
            pltpu.VMEM((2, T, T), F32),
            pltpu.VMEM((2, T, T), F32),
            pltpu.VMEM((2, 1, T), F32),
            pltpu.VMEM((2, 1, T), F32),
            pltpu.VMEM((2, DIFF_DV, T), F32),
        ],
        compiler_params=_params("parallel", "parallel", "arbitrary"),
    )(qk, qk, vt, lam4, subln.reshape(DIFF_DV, 1))


def _group_norm_gate(o, g, gn):
    hid = lax.broadcasted_iota(jnp.int32, (1, RET_W), 1) // RET_DV

    def group_mean(x):
        out = jnp.zeros_like(x)
        for h in range(RET_H):
            mh = hid == h
            s = jnp.sum(jnp.where(mh, x, 0.0), axis=-1, keepdims=True) * (1.0 / RET_DV)
            out = jnp.where(mh, s, out)
        return out

    d = o - group_mean(o)
    var = group_mean(d * d)
    return d * lax.rsqrt(var + EPS) * gn * _silu(g)


def _ret_prompt_kernel(q_ref, k_ref, v_ref, g_ref, gn_ref, dec_ref, cross_ref, tail_ref, gl_ref,
                       o_ref, s_out_ref, S_ref):
    c = pl.program_id(1)

    @pl.when(c == 0)
    def _():
        S_ref[...] = jnp.zeros(S_ref.shape, F32)

    q = q_ref[...]
    k = k_ref[...]
    v = v_ref[...]
    hid = lax.broadcasted_iota(jnp.int32, (1, RET_W), 1) // RET_DV
    Sb = S_ref[...]
    o = jnp.dot(q, Sb.astype(BF16), preferred_element_type=F32) * cross_ref[...]
    for h in range(RET_H):
        mh = hid == h
        qh = jnp.where(mh, q, jnp.zeros_like(q))
        sc = lax.dot_general(qh, k, _NT, preferred_element_type=F32) * dec_ref[h]
        oh = jnp.dot(sc.astype(BF16), v, preferred_element_type=F32)
        o = o + jnp.where(mh, oh, 0.0)
    kt = (k.astype(F32) * tail_ref[...]).astype(BF16)
    upd = lax.dot_general(kt, v, (((0,), (0,)), ((), ())), preferred_element_type=F32)
    rowh = lax.broadcasted_iota(jnp.int32, (RET_W, RET_W), 0) // RET_DK
    colh = lax.broadcasted_iota(jnp.int32, (RET_W, RET_W), 1) // RET_DV
    S_new = gl_ref[...] * Sb + jnp.where(rowh == colh, upd, 0.0)
    S_ref[...] = S_new
    o_ref[...] = _group_norm_gate(o, g_ref[...], gn_ref[...]).astype(BF16)

    @pl.when(c == pl.num_programs(1) - 1)
    def _():
        y = S_new
        for sft in (64, 128, 192):
            y = y + pltpu.roll(S_new, sft, 1)
        s_out_ref[0] = y[:, 0:RET_DV]


def _ret_prompt(ret, gr, gn, B, S, L):
    M = ret.shape[0]
    nc = S // L
    decay, cross, tail, gl = _ret_tables(L)
    rep = lambda a: jnp.repeat(a, RET_DV, axis=-1)
    row = lambda b, c: (b * nc + c, 0)
    const2 = lambda b, c: (0, 0)
    o_r, s_fin = pl.pallas_call(
        _ret_prompt_kernel,
        grid=(B, nc),
        in_specs=[
            pl.BlockSpec((L, RET_W), lambda b, c: (b * nc + c, 0)),
            pl.BlockSpec((L, RET_W), lambda b, c: (b * nc + c, 1)),
            pl.BlockSpec((L, RET_W), lambda b, c: (b * nc + c, 2)),
            pl.BlockSpec((L, RET_W), row),
            pl.BlockSpec((1, RET_W), const2),
            pl.BlockSpec((RET_H, L, L), lambda b, c: (0, 0, 0)),
            pl.BlockSpec((L, RET_W), const2),
            pl.BlockSpec((L, RET_W), const2),
            pl.BlockSpec((1, RET_W), const2),
        ],
        out_specs=[
            pl.BlockSpec((L, RET_W), row),
            pl.BlockSpec((1, RET_W, RET_DV), lambda b, c: (b, 0, 0)),
        ],
        out_shape=[
            jax.ShapeDtypeStruct((M, RET_W), BF16),
            jax.ShapeDtypeStruct((B, RET_W, RET_DV), F32),
        ],
        scratch_shapes=[pltpu.VMEM((RET_W, RET_W), F32)],
        compiler_params=_params("parallel", "arbitrary"),
    )(ret, ret, ret, gr, gn.reshape(1, RET_W), decay, rep(cross), rep(tail), rep(gl[None, :]))
    return o_r, s_fin.reshape(B, RET_H, RET_DK, RET_DV)


def _conv_finish(acc, b, g, bb):
    y = acc + b
    mu = jnp.mean(y, axis=-1, keepdims=True)
    d = y - mu
    var = jnp.mean(d * d, axis=-1, keepdims=True)
    return _silu(d * lax.rsqrt(var + EPS) * g + bb)


def _conv_prompt_kernel(prev_ref, cur_ref, w_ref, b_ref, g_ref, bb_ref, o_ref, buf, sh, *, tc, sub):
    i = pl.program_id(1)
    prev = prev_ref[...]
    buf[0:CONV_HALO, :] = jnp.where(i == 0, jnp.zeros_like(prev), prev)
    buf[CONV_HALO:CONV_HALO + tc, :] = cur_ref[...]
    off = CONV_HALO - (CONV_K - 1)
    cls = [[j for j in range(CONV_K) if (off + j) % 8 == a] for a in range(8)]
    for a, taps in enumerate(cls):
        n = tc + taps[-1] - taps[0]
        sh[a, 0:n, :] = buf[off + taps[0]:off + taps[0] + n, :]

    for r0 in range(0, tc, sub):
        acc = jnp.zeros((sub, CONV_W), F32)
        for a, taps in enumerate(cls):
            for j in taps:
                acc = acc + sh[a, r0 + j - taps[0]:r0 + j - taps[0] + sub, :] * w_ref[j:j + 1, :]
        o_ref[r0:r0 + sub, :] = _conv_finish(acc, b_ref[...], g_ref[...], bb_ref[...]).astype(BF16)


def _conv_prompt(u, w, b, g, bb, B, S, tc, sub=64):
    M = u.shape[0]
    nb = S // tc
    hb = tc // CONV_HALO
    vec = lambda a: a.reshape(1, CONV_W)
    const2 = lambda bi, i: (0, 0)
    return pl.pallas_call(
        functools.partial(_conv_prompt_kernel, tc=tc, sub=sub),
        grid=(B, nb),
        in_specs=[
            pl.BlockSpec((CONV_HALO, CONV_W),
                         lambda bi, i: (jnp.maximum((bi * nb + i) * hb - 1, 0), 0)),
            pl.BlockSpec((tc, CONV_W), lambda bi, i: (bi * nb + i, 0)),
            pl.BlockSpec((CONV_HALO, CONV_W), const2),
            pl.BlockSpec((1, CONV_W), const2),
            pl.BlockSpec((1, CONV_W), const2),
            pl.BlockSpec((1, CONV_W), const2),
        ],
        out_specs=pl.BlockSpec((tc, CONV_W), lambda bi, i: (bi * nb + i, 0)),
        out_shape=jax.ShapeDtypeStruct((M, CONV_W), BF16),
        scratch_shapes=[pltpu.VMEM((CONV_HALO + tc, CONV_W), F32),
                        pltpu.VMEM((8, tc + CONV_HALO - 8, CONV_W), F32)],
        compiler_params=_params("parallel", "arbitrary"),
    )(u, u, jnp.pad(w, ((0, CONV_HALO - CONV_K), (0, 0))), vec(b), vec(g), vec(bb))


def _outproj_mlp_kernel(x_ref, c_ref, od_ref, or_ref, wo_ref, g_ref, wu_ref, wd_ref, gf_ref,
                        o_ref, a_ref, *, final_norm, tf):
    mix = jnp.dot(c_ref[...].astype(BF16), wo_ref[0:CONV_W, :], preferred_element_type=F32)
    mix += jnp.dot(od_ref[...].astype(BF16), wo_ref[CONV_W:CONV_W + DIFF_W, :],
                   preferred_element_type=F32)
    mix += jnp.dot(or_ref[...].astype(BF16), wo_ref[CONV_W + DIFF_W:, :],
                   preferred_element_type=F32)
    xm = x_ref[...] + mix
    h2 = _rms(xm, g_ref[...]).astype(BF16)
    for f0 in range(0, wu_ref.shape[1], tf):
        up = jnp.dot(h2, wu_ref[:, f0:f0 + tf], preferred_element_type=F32)
        a_ref[:, f0:f0 + tf] = jnp.square(jnp.maximum(up, 0.0)).astype(BF16)
    y = xm + jnp.dot(a_ref[...], wd_ref[...], preferred_element_type=F32)
    if final_norm:
        y = _rms(y, gf_ref[...])
    o_ref[...] = y


def _outproj_mlp(x2d, c, od, orr, wo_bf, g, wu_bf, wd_bf, gf, final_norm, tm, tf):
    M, D = x2d.shape
    FF = wu_bf.shape[1]
    row = lambda i: (i, 0)
    resident = lambda shape: pl.BlockSpec(shape, lambda i: (0, 0), pipeline_mode=pl.Buffered(1))
    return pl.pallas_call(
        functools.partial(_outproj_mlp_kernel, final_norm=final_norm, tf=tf),
        grid=(M // tm,),
        in_specs=[
            pl.BlockSpec((tm, D), row),
            pl.BlockSpec((tm, CONV_W), row),
            pl.BlockSpec((tm, DIFF_W), row),
            pl.BlockSpec((tm, RET_W), row),
            resident((D, D)),
            resident((1, D)),
            resident((D, FF)),
            resident((FF, D)),
            resident((1, D)),
        ],
        out_specs=pl.BlockSpec((tm, D), row),
        out_shape=jax.ShapeDtypeStruct((M, D), F32),
        scratch_shapes=[pltpu.VMEM((tm, FF), BF16)],
        compiler_params=_params("parallel"),
    )(x2d, c, od, orr, wo_bf, g.reshape(1, D), wu_bf, wd_bf, gf.reshape(1, D))


def _attn_sample_kernel(pt_ref, q_ref, kn_ref, vn_ref, lam_ref, sub_ref, *rest,
                        P, TD, page, lam_init):
    kp = rest[0:P]
    vp = rest[P:2 * P]
    o_ref, wq, m_ref, l_ref, acc_ref = rest[2 * P:]
    R = 8 * TD
    s_id = pl.program_id(1)

    @pl.when(s_id == 0)
    def _():
        q = q_ref[0]
        hm = lax.broadcasted_iota(jnp.int32, (8, 512), 0)
        lh = lax.broadcasted_iota(jnp.int32, (8, 512), 1) // DIFF_DQK
        for t in range(TD):
            wq[8 * t:8 * t + 8, :] = jnp.where(hm == lh, jnp.broadcast_to(q[t:t + 1, :], (8, 512)), 0.0)
        m_ref[...] = jnp.full(m_ref.shape, -jnp.inf, F32)
        l_ref[...] = jnp.zeros(l_ref.shape, F32)
        acc_ref[...] = jnp.zeros(acc_ref.shape, F32)

    def update(s, vmat):
        m_prev = m_ref[...]
        m_new = jnp.maximum(m_prev, jnp.max(s, axis=-1, keepdims=True))
        a = jnp.exp2(m_prev - m_new)
        p = jnp.exp2(s - m_new)
        l_ref[...] = a * l_ref[...] + jnp.sum(p, axis=-1, keepdims=True)
        acc_ref[...] = a * acc_ref[...] + jnp.dot(p.astype(BF16), vmat, preferred_element_type=F32)
        m_ref[...] = m_new

    wqb = wq[...].astype(BF16)
    kcat = jnp.concatenate([r[0].astype(BF16) for r in kp], axis=1)

    def v_page(r):
        heads = [r[0, pl.ds(h, page, stride=DIFF_H), :] for h in range(DIFF_H)]
        return jnp.concatenate(heads, axis=1).astype(BF16)

    vcat = jnp.concatenate([v_page(r) for r in vp], axis=0)
    update(jnp.dot(wqb, kcat, preferred_element_type=F32), vcat)

    @pl.when(s_id == pl.num_programs(1) - 1)
    def _():
        pad = jnp.zeros((16 - TD, 512), F32)
        kn = jnp.concatenate([kn_ref[0], pad], axis=0).astype(BF16)
        vn = jnp.concatenate([vn_ref[0], pad], axis=0).astype(BF16)
        s2 = lax.dot_general(wqb, kn, _NT, preferred_element_type=F32)
        tq = lax.broadcasted_iota(jnp.int32, (R, 16), 0) // 8
        tk = lax.broadcasted_iota(jnp.int32, (R, 16), 1)
        update(jnp.where(tk <= tq, s2, -jnp.inf), vn)

        lam = _lam(lam_ref, lam_init)
        o = acc_ref[...] / l_ref[...]
        hm = lax.broadcasted_iota(jnp.int32, (8, 512), 0)
        lh = lax.broadcasted_iota(jnp.int32, (8, 512), 1) // DIFF_DV
        coef = jnp.where(hm // 2 == lh, jnp.where(hm % 2 == 0, 1.0, -lam), 0.0)
        for t in range(TD):
            d = jnp.sum(o[8 * t:8 * t + 8, :] * coef, axis=0, keepdims=True)
            for h in range(DIFF_H):
                sl = slice(DIFF_DV * h, DIFF_DV * (h + 1))
                o_ref[0, t:t + 1, sl] = _rms(d[:, sl], sub_ref[...]) * (1.0 - lam_init)


def _attn_sample(q3, kn3, vn3, cache_kt3, cache_v3, pt_flat, lam4, subln, lam_init, layer, n_phys,
                 n_pages, P):
    DB, TD, _ = q3.shape
    page = cache_kt3.shape[2]
    base = layer * n_phys

    def page_map(j):
        return lambda b, s, pt: (base + pt[b * n_pages + s * P + j], 0, 0)

    tok = pl.BlockSpec((1, TD, 512), lambda b, s, pt: (b, 0, 0))
    in_specs = [
        tok, tok, tok,
        pl.BlockSpec((4, DIFF_DQK), lambda b, s, pt: (0, 0)),
        pl.BlockSpec((1, DIFF_DV), lambda b, s, pt: (0, 0)),
    ]
    in_specs += [pl.BlockSpec((1, 512, page), page_map(j)) for j in range(P)]
    in_specs += [pl.BlockSpec((1, page * DIFF_H, DIFF_DV), page_map(j)) for j in range(P)]
    R = 8 * TD
    grid_spec = pltpu.PrefetchScalarGridSpec(
        num_scalar_prefetch=1,
        grid=(DB, n_pages // P),
        in_specs=in_specs,
        out_specs=pl.BlockSpec((1, TD, 512), lambda b, s, pt: (b, 0, 0)),
        scratch_shapes=[
            pltpu.VMEM((R, 512), F32),
            pltpu.VMEM((R, 1), F32),
            pltpu.VMEM((R, 1), F32),
            pltpu.VMEM((R, 512), F32),
        ],
    )
    return pl.pallas_call(
        functools.partial(_attn_sample_kernel, P=P, TD=TD, page=page, lam_init=lam_init),
        grid_spec=grid_spec,
        out_shape=jax.ShapeDtypeStruct((DB, TD, 512), F32),
        compiler_params=_params("parallel", "arbitrary"),
    )(pt_flat, q3, kn3, vn3, lam4, subln.reshape(1, DIFF_DV),
      *([cache_kt3] * P), *([cache_v3] * P))


def _ret_sample_kernel(q_ref, k_ref, kT_ref, v_ref, g_ref, gn_ref, S_ref, dec_ref, cross_ref,
                       tail_ref, gl_ref, o_ref, s_out_ref, *, TD):
    q = q_ref[...]
    k = k_ref[...]
    v = v_ref[...]
    S = S_ref[...]
    sc = jnp.einsum('gtd,gsd->gts', q, k, preferred_element_type=F32) * dec_ref[...]
    o = jnp.einsum('gtd,gde->gte', q, S, preferred_element_type=F32) * cross_ref[...]
    for s in range(TD):
        o = o + sc[:, :, s:s + 1] * v[:, s:s + 1, :]
    kT = kT_ref[...]
    vt = v * tail_ref[...]
    S_new = gl_ref[...] * S
    for s in range(TD):
        S_new = S_new + kT[:, :, s:s + 1] * vt[:, s:s + 1, :]
    s_out_ref[...] = S_new
    mu = jnp.mean(o, axis=-1, keepdims=True)
    d = o - mu
    var = jnp.mean(d * d, axis=-1, keepdims=True)
    o_ref[...] = d * lax.rsqrt(var + EPS) * gn_ref[...] * _silu(g_ref[...])


def _ret_sample(q, k, v, g, gn, S_prev, TD, Gb):
    G = q.shape[0]
    lg = jnp.tile(_log_gamma(), G // RET_H)
    decay, cross, tail, gl = _ret_tables(TD)
    dec_g = jnp.tile(decay, (G // RET_H, 1, 1))
    cross_g = jnp.tile(cross.T, (G // RET_H, 1))[:, :, None]
    tail_g = jnp.tile(tail.T, (G // RET_H, 1))[:, :, None]
    gl_g = jnp.tile(gl, G // RET_H)[:, None, None]
    del lg
    kT = jnp.swapaxes(k, 1, 2)
    b3 = lambda shp: pl.BlockSpec((Gb,) + shp, lambda i: (i, 0, 0))
    return pl.pallas_call(
        functools.partial(_ret_sample_kernel, TD=TD),
        grid=(G // Gb,),
        in_specs=[b3((TD, 64)), b3((TD, 64)), b3((64, TD)), b3((TD, 64)), b3((TD, 64)), b3((1, 64)),
                  b3((64, 64)), b3((TD, TD)), b3((TD, 1)), b3((TD, 1)), b3((1, 1))],
        out_specs=[b3((TD, 64)), b3((64, 64))],
        out_shape=[jax.ShapeDtypeStruct((G, TD, 64), F32), jax.ShapeDtypeStruct((G, 64, 64), F32)],
        compiler_params=_params("parallel"),
    )(q, k, kT, v, g, gn, S_prev, dec_g, cross_g, tail_g, gl_g)


def _conv_sample_kernel(ext_ref, w_ref, b_ref, g_ref, bb_ref, o_ref, *, TD):
    acc = jnp.zeros(o_ref.shape, F32)
    for j in range(CONV_K):
        acc = acc + ext_ref[:, j:j + TD, :] * w_ref[j:j + 1, :]
    o_ref[...] = _conv_finish(acc, b_ref[...], g_ref[...], bb_ref[...])


def _conv_sample(ext, w, b, g, bb, TD):
    DB = ext.shape[0]
    vec = lambda a: a.reshape(1, CONV_W)
    return pl.pallas_call(
        functools.partial(_conv_sample_kernel, TD=TD),
        out_shape=jax.ShapeDtypeStruct((DB, TD, CONV_W), F32),
    )(ext, w, vec(b), vec(g), vec(bb))


def _pick(n, pref):
    t = min(n, pref)
    while n % t:
        t //= 2
    return t


def kernel(x_prompt, x_sample, cache_k, cache_v, state_conv, state_ret, page_table, norm_mix, w_in, conv_w, conv_b, conv_ln_g, conv_ln_b, lam_q1, lam_k1, lam_q2, lam_k2, diff_subln, ret_gn, w_out, norm_mlp, w_up, w_down, norm_final):
    B, S, D = x_prompt.shape
    DB, TD, _ = x_sample.shape
    depth = w_in.shape[0]
    n_phys, page = cache_k.shape[1], cache_k.shape[2]
    n_pages = page_table.shape[1]
    past_len = n_pages * page
    Mp, Ms = B * S, DB * TD

    T_attn = _pick(S, 512)
    tm_in = T_attn
    L_ret = _pick(S, 256)
    tc_conv = _pick(S, 512)
    tm_mlp = _pick(Mp, 512)
    tf_mlp = _pick(w_up.shape[2], 1024)
    P = _pick(n_pages, 16)
    Gb = _pick(DB * RET_H, 8)

    w_in_bf = w_in.astype(BF16)
    w_out_bf = w_out.astype(BF16)
    w_up_bf = w_up.astype(BF16)
    w_down_bf = w_down.astype(BF16)

    tab_p = _rope_table(jnp.arange(S))
    tab_s = jnp.tile(_rope_table(past_len + jnp.arange(TD)), (DB, 1))
    lam_all = jnp.stack([lam_q1, lam_k1, lam_q2, lam_k2], axis=1)
    w_vt_bf = jnp.swapaxes(w_in[:, :, _OFF_VD:_OFF_QR], 1, 2).astype(BF16)
    cache_kt3 = jnp.transpose(cache_k, (0, 1, 3, 4, 2)).reshape(
        depth * n_phys, 2 * DIFF_H * DIFF_DQK, page)
    cache_v3 = cache_v.reshape(depth * n_phys, page * DIFF_H, DIFF_DV)
    pt_flat = page_table.reshape(-1)

    xp = x_prompt.reshape(Mp, D)
    xs = x_sample.reshape(Ms, D)
    kp, vp, cp, rp, ks_, vs_, cs_, rs_ = [], [], [], [], [], [], [], []
    for l in range(depth):
        lam_init = 0.8 - 0.6 * math.exp(-0.3 * l)
        last = l == depth - 1

        u, qk, k_d, v_d, vt, ret, g_r = _inproj(xp, norm_mix[l], w_in_bf[l], w_vt_bf[l], tab_p, tm_in)
        o_d = _attn_prompt(qk, vt, lam_all[l], diff_subln[l], lam_init, B, S, T_attn)
        o_r, s_fin = _ret_prompt(ret, g_r, ret_gn[l], B, S, L_ret)
        c = _conv_prompt(u, conv_w[l], conv_b[l], conv_ln_g[l], conv_ln_b[l], B, S, tc_conv)
        xp = _outproj_mlp(xp, c, o_d, o_r, w_out_bf[l], norm_mlp[l], w_up_bf[l], w_down_bf[l],
                          norm_final, last, tm_mlp, tf_mlp)
        kp.append(k_d.reshape(B, S, 2 * DIFF_H, DIFF_DQK))
        vp.append(v_d.reshape(B, S, DIFF_H, DIFF_DV))
        cp.append(u.reshape(B, S, CONV_W)[:, S - (CONV_K - 1):])
        rp.append(s_fin)

        u, qk, k_d, v_d, _, ret, g_r = _inproj(xs, norm_mix[l], w_in_bf[l], w_vt_bf[l], tab_s, Ms)
        q3 = qk[:, 0:512].astype(F32).reshape(DB, TD, 512)
        o_d = _attn_sample(q3, k_d.reshape(DB, TD, 512), v_d.reshape(DB, TD, 512), cache_kt3,
                           cache_v3, pt_flat, lam_all[l], diff_subln[l], lam_init,
                           l, n_phys, n_pages, P)
        to_g = lambda a: a.astype(F32).reshape(DB, TD, RET_H, 64).swapaxes(1, 2).reshape(
            DB * RET_H, TD, 64)
        gn_g = jnp.tile(ret_gn[l].reshape(RET_H, 1, RET_DV), (DB, 1, 1))
        o_r, s_new = _ret_sample(to_g(ret[:, 0:256]), to_g(ret[:, 256:512]), to_g(ret[:, 512:768]),
                                 to_g(g_r), gn_g, state_ret[l].reshape(DB * RET_H, RET_DK, RET_DV),
                                 TD, Gb)
        o_r = o_r.reshape(DB, RET_H, TD, RET_DV).swapaxes(1, 2).reshape(Ms, RET_W)
        ext = jnp.concatenate([state_conv[l], u.reshape(DB, TD, CONV_W)], axis=1)
        c = _conv_sample(ext, conv_w[l], conv_b[l], conv_ln_g[l], conv_ln_b[l], TD)
        xs = _outproj_mlp(xs, c.reshape(Ms, CONV_W), o_d.reshape(Ms, DIFF_W), o_r, w_out_bf[l],
                          norm_mlp[l], w_up_bf[l], w_down_bf[l], norm_final, last, Ms, tf_mlp)
        ks_.append(k_d.reshape(DB, TD, 2 * DIFF_H, DIFF_DQK))
        vs_.append(v_d.reshape(DB, TD, DIFF_H, DIFF_DV))
        cs_.append(ext[:, TD:])
        rs_.append(s_new.reshape(DB, RET_H, RET_DK, RET_DV))

    return (xp.reshape(B, S, D), xs.reshape(DB, TD, D),
            jnp.stack(kp), jnp.stack(vp), jnp.stack(cp), jnp.stack(rp),
            jnp.stack(ks_), jnp.stack(vs_), jnp.stack(cs_), jnp.stack(rs_))
```

```python
import functools
import math

import jax
import jax.numpy as jnp
from jax import lax
from jax.experimental import pallas as pl
from jax.experimental.pallas import tpu as pltpu

F32 = jnp.float32
BF16 = jnp.bfloat16

CONV_W = 256
CONV_K = 31
DIFF_H = 4
DIFF_DQK = 64
DIFF_DV = 128
DIFF_W = DIFF_H * DIFF_DV
RET_H = 4
RET_DK = 64
RET_DV = 64
RET_W = RET_H * RET_DV
ROPE_THETA = 500000.0
ROPE_ROT = DIFF_DQK // 4
RET_THETA = 10000.0
EPS = 1e-5
LANES = 128
CONV_HALO = 32
VMEM_LIMIT = 56 * 1024 * 1024

_OFF_C, _OFF_QD, _OFF_KD, _OFF_VD, _OFF_QR, _OFF_KR, _OFF_VR, _OFF_GR, _D_IN = (
    0, 512, 1024, 1536, 2048, 2304, 2560, 2816, 3072)

_NT = (((1,), (1,)), ((), ()))

Q_SCALE = DIFF_DQK ** -0.5 * math.log2(math.e)


def _params(*sem):
    return pltpu.CompilerParams(dimension_semantics=sem, vmem_limit_bytes=VMEM_LIMIT)


def _rms(x, g):
    return x * lax.rsqrt(jnp.mean(x * x, axis=-1, keepdims=True) + EPS) * g


def _silu(x):
    return x / (1.0 + jnp.exp(-x))


def _lam(lam_ref, lam_init):
    s1 = jnp.sum(lam_ref[0:1, :] * lam_ref[1:2, :], axis=-1, keepdims=True)
    s2 = jnp.sum(lam_ref[2:3, :] * lam_ref[3:4, :], axis=-1, keepdims=True)
    return jnp.exp(s1) - jnp.exp(s2) + lam_init


def _rope_table(pos):
    posf = pos.astype(F32)[:, None]
    n = pos.shape[0]

    def one(n_rot, theta, hd):
        half = n_rot // 2
        inv = 1.0 / (theta ** (jnp.arange(half, dtype=F32) * 2.0 / n_rot))
        ang = posf * inv[None, :]
        cos, sin = jnp.cos(ang), jnp.sin(ang)
        c = jnp.concatenate([cos, cos, jnp.ones((n, hd - n_rot), F32)], axis=-1)
        s = jnp.concatenate([-sin, sin, jnp.zeros((n, hd - n_rot), F32)], axis=-1)
        return jnp.tile(c, (1, LANES // hd)), jnp.tile(s, (1, LANES // hd))

    cd, sd = one(ROPE_ROT, ROPE_THETA, DIFF_DQK)
    cr, sr = one(RET_DK, RET_THETA, RET_DK)
    return jnp.concatenate([cd, sd, cr, sr], axis=-1)


def _log_gamma():
    return jnp.log(1.0 - jnp.exp2(-5.0 - jnp.arange(RET_H, dtype=F32)))


def _ret_tables(L):
    lg = _log_gamma()
    idx = jnp.arange(L, dtype=F32)
    diff = idx[:, None] - idx[None, :]
    decay = jnp.where(diff >= 0, jnp.exp(jnp.maximum(diff, 0.0)[None] * lg[:, None, None]), 0.0)
    cross = jnp.exp((idx[:, None] + 1.0) * lg[None, :])
    tail = jnp.exp((L - 1.0 - idx)[:, None] * lg[None, :])
    gl = jnp.exp(L * lg)
    return decay, cross, tail, gl


def _inproj_kernel(x_ref, g_ref, w_ref, wvt_ref, t_ref, u_ref, qk_ref, k_ref, v_ref, vt_ref,
                   ret_ref, gr_ref):
    h = _rms(x_ref[...], g_ref[...]).astype(BF16)

    def proj(a, b):
        return jnp.dot(h, w_ref[:, a:b], preferred_element_type=F32)

    lane = lax.broadcasted_iota(jnp.int32, (1, LANES), 1) % 64

    def rope(xg, c, s, half):
        partner = jnp.where(lane < half, pltpu.roll(xg, LANES - half, 1), pltpu.roll(xg, half, 1))
        return xg * c + partner * s

    cd, sd = t_ref[:, 0:128], t_ref[:, 128:256]
    cr, sr = t_ref[:, 256:384], t_ref[:, 384:512]

    c_ag = proj(_OFF_C, _OFF_QD)
    u_ref[...] = c_ag[:, :CONV_W] / (1.0 + jnp.exp(-c_ag[:, CONV_W:]))

    q = proj(_OFF_QD, _OFF_KD)
    for j in range(4):
        sl = slice(LANES * j, LANES * (j + 1))
        qk_ref[:, sl] = (rope(q[:, sl], cd, sd, ROPE_ROT // 2) * Q_SCALE).astype(BF16)
    k = proj(_OFF_KD, _OFF_VD)
    for j in range(4):
        sl = slice(LANES * j, LANES * (j + 1))
        kj = rope(k[:, sl], cd, sd, ROPE_ROT // 2)
        k_ref[:, sl] = kj
        qk_ref[:, 512 + LANES * j:512 + LANES * (j + 1)] = kj.astype(BF16)
    v = proj(_OFF_VD, _OFF_QR)
    tm = v.shape[0]
    for j in range(DIFF_H):
        v_ref[pl.ds(j, tm, stride=DIFF_H), :] = v[:, DIFF_DV * j:DIFF_DV * (j + 1)]
    vt_ref[0] = lax.dot_general(wvt_ref[...], h, _NT, preferred_element_type=F32).astype(BF16)

    r = proj(_OFF_QR, _OFF_GR)
    for j in range(2):
        sl = slice(LANES * j, LANES * (j + 1))
        ret_ref[:, sl] = rope(r[:, sl], cr, sr, RET_DK // 2).astype(BF16)
    for j in range(2, 4):
        sl = slice(LANES * j, LANES * (j + 1))
        ret_ref[:, sl] = (rope(r[:, sl], cr, sr, RET_DK // 2) * (RET_DK ** -0.5)).astype(BF16)
    ret_ref[:, 512:768] = r[:, 512:768].astype(BF16)
    gr_ref[...] = proj(_OFF_GR, _D_IN)


def _inproj(x2d, g, w_bf, wvt_bf, table, tm):
    M, D = x2d.shape
    n_t = table.shape[0] // tm
    row = lambda i: (i, 0)
    return pl.pallas_call(
        _inproj_kernel,
        grid=(M // tm,),
        in_specs=[
            pl.BlockSpec((tm, D), row),
            pl.BlockSpec((1, D), lambda i: (0, 0)),
            pl.BlockSpec((D, _D_IN), lambda i: (0, 0)),
            pl.BlockSpec((DIFF_W, D), lambda i: (0, 0)),
            pl.BlockSpec((tm, 512), lambda i: (i % n_t, 0)),
        ],
        out_specs=[
            pl.BlockSpec((tm, CONV_W), row),
            pl.BlockSpec((tm, 1024), row),
            pl.BlockSpec((tm, 512), row),
            pl.BlockSpec((tm * DIFF_H, DIFF_DV), row),
            pl.BlockSpec((1, DIFF_W, tm), lambda i: (i, 0, 0)),
            pl.BlockSpec((tm, 768), row),
            pl.BlockSpec((tm, RET_W), row),
        ],
        out_shape=[
            jax.ShapeDtypeStruct((M, CONV_W), F32),
            jax.ShapeDtypeStruct((M, 1024), BF16),
            jax.ShapeDtypeStruct((M, 512), F32),
            jax.ShapeDtypeStruct((M * DIFF_H, DIFF_DV), F32),
            jax.ShapeDtypeStruct((M // tm, DIFF_W, tm), BF16),
            jax.ShapeDtypeStruct((M, 768), BF16),
            jax.ShapeDtypeStruct((M, RET_W), F32),
        ],
        compiler_params=_params("parallel"),
    )(x2d, g.reshape(1, D), w_bf, wvt_bf, table)


def _attn_prompt_kernel(q_ref, k_ref, vt_ref, lam_ref, sub_ref, o_ref,
                        qs, sa_ref, sb_ref, m_ref, l_ref, acc_ref, *, T, NC, nq, lam_init):
    C = T // NC
    S = nq * T
    q = q_ref[...]
    lane = lax.broadcasted_iota(jnp.int32, (S, LANES), 1)
    qs[0] = jnp.where(lane < DIFF_DQK, q, jnp.zeros_like(q))
    qs[1] = jnp.where(lane >= DIFF_DQK, q, jnp.zeros_like(q))
    lam = _lam(lam_ref, lam_init)
    chains = [(mp, c) for mp in range(2) for c in range(NC)]

    def scores(qi, ki, s_ref):
        k = k_ref[ki * T:(ki + 1) * T, :]
        for mp, c in chains:
            cols = slice(c * C, (c + 1) * C)
            s_ref[mp, :, cols] = lax.dot_general(
                k, qs[mp, qi * T + c * C:qi * T + (c + 1) * C, :], _NT, preferred_element_type=F32)

    def accumulate(qi, ki, s_ref):
        diagonal = ki == qi
        for mp, c in chains:
            cols = slice(c * C, (c + 1) * C)
            nk = (c + 1) * C if diagonal else T
            s = s_ref[mp, 0:nk, cols]
            if diagonal:
                key = lax.broadcasted_iota(jnp.int32, (nk, C), 0)
                qq = lax.broadcasted_iota(jnp.int32, (nk, C), 1) + c * C
                s = jnp.where(key <= qq, s, -jnp.inf)
            pv = lambda p: jnp.dot(vt_ref[ki, :, 0:nk], p.astype(BF16),
                                   preferred_element_type=F32)
            if ki == 0:
                m_new = jnp.max(s, axis=0, keepdims=True)
                p = jnp.exp2(s - m_new)
                l_ref[mp, :, cols] = jnp.sum(p, axis=0, keepdims=True)
                acc_ref[mp, :, cols] = pv(p)
            else:
                m_prev = m_ref[mp, :, cols]
                m_new = jnp.maximum(m_prev, jnp.max(s, axis=0, keepdims=True))
                a = jnp.exp2(m_prev - m_new)
                p = jnp.exp2(s - m_new)
                l_ref[mp, :, cols] = a * l_ref[mp, :, cols] + jnp.sum(p, axis=0, keepdims=True)
                acc_ref[mp, :, cols] = a * acc_ref[mp, :, cols] + pv(p)
            m_ref[mp, :, cols] = m_new

    def finish(qi):
        d = acc_ref[0] / l_ref[0] - lam * (acc_ref[1] / l_ref[1])
        r = lax.rsqrt(jnp.mean(d * d, axis=0, keepdims=True) + EPS)
        y = d * r * (sub_ref[...] * (1.0 - lam_init))
        o_ref[qi * T:(qi + 1) * T, :] = y.T.astype(BF16)

    pairs = [(qi, ki) for qi in range(nq) for ki in range(qi + 1)]
    bufs = (sa_ref, sb_ref)
    scores(*pairs[0], bufs[0])
    for t, (qi, ki) in enumerate(pairs):
        if t + 1 < len(pairs):
            scores(*pairs[t + 1], bufs[(t + 1) % 2])
        accumulate(qi, ki, bufs[t % 2])
        if ki == qi:
            finish(qi)


def _attn_prompt(qk, vt, lam4, subln, lam_init, B, S, T, NC=2):
    M = qk.shape[0]
    nq = S // T
    return pl.pallas_call(
        functools.partial(_attn_prompt_kernel, T=T, NC=NC, nq=nq, lam_init=lam_init),
        grid=(B, DIFF_H),
        in_specs=[
            pl.BlockSpec((S, LANES), lambda b, h: (b, h)),
            pl.BlockSpec((S, LANES), lambda b, h: (b, DIFF_H + h)),
            pl.BlockSpec((nq, DIFF_DV, T), lambda b, h: (b, h, 0)),
            pl.BlockSpec((4, DIFF_DQK), lambda b, h: (0, 0)),
            pl.BlockSpec((DIFF_DV, 1), lambda b, h: (0, 0)),
        ],
        out_specs=pl.BlockSpec((S, LANES), lambda b, h: (b, h)),
        out_shape=jax.ShapeDtypeStruct((M, DIFF_W), BF16),
        scratch_shapes=[
            pltpu.VMEM((2, S, LANES), BF16),
            pltpu.VMEM((2, T, T), F32),
            pltpu.VMEM((2, T, T), F32),
            pltpu.VMEM((2, 1, T), F32),
            pltpu.VMEM((2, 1, T), F32),
            pltpu.VMEM((2, DIFF_DV, T), F32),
        ],
        compiler_params=_params("parallel", "parallel"),
    )(qk, qk, vt, lam4, subln.reshape(DIFF_DV, 1))


def _group_norm_gate(o, g, gn):
    hid = lax.broadcasted_iota(jnp.int32, (1, RET_W), 1) // RET_DV

    def group_mean(x):
        out = jnp.zeros_like(x)
        for h in range(RET_H):
            mh = hid == h
            s = jnp.sum(jnp.where(mh, x, 0.0), axis=-1, keepdims=True) * (1.0 / RET_DV)
            out = jnp.where(mh, s, out)
        return out

    d = o - group_mean(o)
    var = group_mean(d * d)
    return d * lax.rsqrt(var + EPS) * gn * _silu(g)


def _ret_prompt_kernel(q_ref, k_ref, v_ref, g_ref, gn_ref, dec_ref, cross_ref, tail_ref, gl_ref,
                       o_ref, s_out_ref, S_ref):
    c = pl.program_id(1)

    @pl.when(c == 0)
    def _():
        S_ref[...] = jnp.zeros(S_ref.shape, F32)

    q = q_ref[...]
    k = k_ref[...]
    v = v_ref[...]
    hid = lax.broadcasted_iota(jnp.int32, (1, RET_W), 1) // RET_DV
    Sb = S_ref[...]
    o = jnp.dot(q, Sb.astype(BF16), preferred_element_type=F32) * cross_ref[...]
    for h in range(RET_H):
        mh = hid == h
        qh = jnp.where(mh, q, jnp.zeros_like(q))
        sc = lax.dot_general(qh, k, _NT, preferred_element_type=F32) * dec_ref[h]
        oh = jnp.dot(sc.astype(BF16), v, preferred_element_type=F32)
        o = o + jnp.where(mh, oh, 0.0)
    kt = (k.astype(F32) * tail_ref[...]).astype(BF16)
    upd = lax.dot_general(kt, v, (((0,), (0,)), ((), ())), preferred_element_type=F32)
    rowh = lax.broadcasted_iota(jnp.int32, (RET_W, RET_W), 0) // RET_DK
    colh = lax.broadcasted_iota(jnp.int32, (RET_W, RET_W), 1) // RET_DV
    S_new = gl_ref[...] * Sb + jnp.where(rowh == colh, upd, 0.0)
    S_ref[...] = S_new
    o_ref[...] = _group_norm_gate(o, g_ref[...], gn_ref[...]).astype(BF16)

    @pl.when(c == pl.num_programs(1) - 1)
    def _():
        y = S_new
        for sft in (64, 128, 192):
            y = y + pltpu.roll(S_new, sft, 1)
        s_out_ref[0] = y[:, 0:RET_DV]


def _ret_prompt(ret, gr, gn, B, S, L):
    M = ret.shape[0]
    nc = S // L
    decay, cross, tail, gl = _ret_tables(L)
    rep = lambda a: jnp.repeat(a, RET_DV, axis=-1)
    row = lambda b, c: (b * nc + c, 0)
    const2 = lambda b, c: (0, 0)
    o_r, s_fin = pl.pallas_call(
        _ret_prompt_kernel,
        grid=(B, nc),
        in_specs=[
            pl.BlockSpec((L, RET_W), lambda b, c: (b * nc + c, 0)),
            pl.BlockSpec((L, RET_W), lambda b, c: (b * nc + c, 1)),
            pl.BlockSpec((L, RET_W), lambda b, c: (b * nc + c, 2)),
            pl.BlockSpec((L, RET_W), row),
            pl.BlockSpec((1, RET_W), const2),
            pl.BlockSpec((RET_H, L, L), lambda b, c: (0, 0, 0)),
            pl.BlockSpec((L, RET_W), const2),
            pl.BlockSpec((L, RET_W), const2),
            pl.BlockSpec((1, RET_W), const2),
        ],
        out_specs=[
            pl.BlockSpec((L, RET_W), row),
            pl.BlockSpec((1, RET_W, RET_DV), lambda b, c: (b, 0, 0)),
        ],
        out_shape=[
            jax.ShapeDtypeStruct((M, RET_W), BF16),
            jax.ShapeDtypeStruct((B, RET_W, RET_DV), F32),
        ],
        scratch_shapes=[pltpu.VMEM((RET_W, RET_W), F32)],
        compiler_params=_params("parallel", "arbitrary"),
    )(ret, ret, ret, gr, gn.reshape(1, RET_W), decay, rep(cross), rep(tail), rep(gl[None, :]))
    return o_r, s_fin.reshape(B, RET_H, RET_DK, RET_DV)


def _conv_finish(acc, b, g, bb):
    y = acc + b
    mu = jnp.mean(y, axis=-1, keepdims=True)
    d = y - mu
    var = jnp.mean(d * d, axis=-1, keepdims=True)
    return _silu(d * lax.rsqrt(var + EPS) * g + bb)


def _conv_prompt_kernel(prev_ref, cur_ref, w_ref, b_ref, g_ref, bb_ref, o_ref, buf, sh, *, tc, sub):
    i = pl.program_id(1)
    prev = prev_ref[...]
    buf[0:CONV_HALO, :] = jnp.where(i == 0, jnp.zeros_like(prev), prev)
    buf[CONV_HALO:CONV_HALO + tc, :] = cur_ref[...]
    off = CONV_HALO - (CONV_K - 1)
    cls = [[j for j in range(CONV_K) if (off + j) % 8 == a] for a in range(8)]
    for a, taps in enumerate(cls):
        n = tc + taps[-1] - taps[0]
        sh[a, 0:n, :] = buf[off + taps[0]:off + taps[0] + n, :]

    for r0 in range(0, tc, sub):
        acc = jnp.zeros((sub, CONV_W), F32)
        for a, taps in enumerate(cls):
            for j in taps:
                acc = acc + sh[a, r0 + j - taps[0]:r0 + j - taps[0] + sub, :] * w_ref[j:j + 1, :]
        o_ref[r0:r0 + sub, :] = _conv_finish(acc, b_ref[...], g_ref[...], bb_ref[...]).astype(BF16)


def _conv_prompt(u, w, b, g, bb, B, S, tc, sub=64):
    M = u.shape[0]
    nb = S // tc
    hb = tc // CONV_HALO
    vec = lambda a: a.reshape(1, CONV_W)
    const2 = lambda bi, i: (0, 0)
    return pl.pallas_call(
        functools.partial(_conv_prompt_kernel, tc=tc, sub=sub),
        grid=(B, nb),
        in_specs=[
            pl.BlockSpec((CONV_HALO, CONV_W),
                         lambda bi, i: (jnp.maximum((bi * nb + i) * hb - 1, 0), 0)),
            pl.BlockSpec((tc, CONV_W), lambda bi, i: (bi * nb + i, 0)),
            pl.BlockSpec((CONV_HALO, CONV_W), const2),
            pl.BlockSpec((1, CONV_W), const2),
            pl.BlockSpec((1, CONV_W), const2),
            pl.BlockSpec((1, CONV_W), const2),
        ],
        out_specs=pl.BlockSpec((tc, CONV_W), lambda bi, i: (bi * nb + i, 0)),
        out_shape=jax.ShapeDtypeStruct((M, CONV_W), BF16),
        scratch_shapes=[pltpu.VMEM((CONV_HALO + tc, CONV_W), F32),
                        pltpu.VMEM((8, tc + CONV_HALO - 8, CONV_W), F32)],
        compiler_params=_params("parallel", "arbitrary"),
    )(u, u, jnp.pad(w, ((0, CONV_HALO - CONV_K), (0, 0))), vec(b), vec(g), vec(bb))


def _outproj_mlp_kernel(x_ref, c_ref, od_ref, or_ref, wo_ref, g_ref, wu_ref, wd_ref, gf_ref,
                        o_ref, a_ref, *, final_norm, tf):
    mix = jnp.dot(c_ref[...].astype(BF16), wo_ref[0:CONV_W, :], preferred_element_type=F32)
    mix += jnp.dot(od_ref[...].astype(BF16), wo_ref[CONV_W:CONV_W + DIFF_W, :],
                   preferred_element_type=F32)
    mix += jnp.dot(or_ref[...].astype(BF16), wo_ref[CONV_W + DIFF_W:, :],
                   preferred_element_type=F32)
    xm = x_ref[...] + mix
    h2 = _rms(xm, g_ref[...]).astype(BF16)
    for f0 in range(0, wu_ref.shape[1], tf):
        up = jnp.dot(h2, wu_ref[:, f0:f0 + tf], preferred_element_type=F32)
        a_ref[:, f0:f0 + tf] = jnp.square(jnp.maximum(up, 0.0)).astype(BF16)
    y = xm + jnp.dot(a_ref[...], wd_ref[...], preferred_element_type=F32)
    if final_norm:
        y = _rms(y, gf_ref[...])
    o_ref[...] = y


def _outproj_mlp(x2d, c, od, orr, wo_bf, g, wu_bf, wd_bf, gf, final_norm, tm, tf):
    M, D = x2d.shape
    FF = wu_bf.shape[1]
    row = lambda i: (i, 0)
    resident = lambda shape: pl.BlockSpec(shape, lambda i: (0, 0), pipeline_mode=pl.Buffered(1))
    return pl.pallas_call(
        functools.partial(_outproj_mlp_kernel, final_norm=final_norm, tf=tf),
        grid=(M // tm,),
        in_specs=[
            pl.BlockSpec((tm, D), row),
            pl.BlockSpec((tm, CONV_W), row),
            pl.BlockSpec((tm, DIFF_W), row),
            pl.BlockSpec((tm, RET_W), row),
            resident((D, D)),
            resident((1, D)),
            resident((D, FF)),
            resident((FF, D)),
            resident((1, D)),
        ],
        out_specs=pl.BlockSpec((tm, D), row),
        out_shape=jax.ShapeDtypeStruct((M, D), F32),
        scratch_shapes=[pltpu.VMEM((tm, FF), BF16)],
        compiler_params=_params("parallel"),
    )(x2d, c, od, orr, wo_bf, g.reshape(1, D), wu_bf, wd_bf, gf.reshape(1, D))


def _attn_sample_kernel(pt_ref, q_ref, kn_ref, vn_ref, lam_ref, sub_ref, *rest,
                        P, TD, page, lam_init):
    kp = rest[0:P]
    vp = rest[P:2 * P]
    o_ref, wq, m_ref, l_ref, acc_ref = rest[2 * P:]
    R = 8 * TD
    s_id = pl.program_id(1)

    @pl.when(s_id == 0)
    def _():
        q = q_ref[0]
        hm = lax.broadcasted_iota(jnp.int32, (8, 512), 0)
        lh = lax.broadcasted_iota(jnp.int32, (8, 512), 1) // DIFF_DQK
        for t in range(TD):
            wq[8 * t:8 * t + 8, :] = jnp.where(hm == lh, jnp.broadcast_to(q[t:t + 1, :], (8, 512)), 0.0)
        m_ref[...] = jnp.full(m_ref.shape, -jnp.inf, F32)
        l_ref[...] = jnp.zeros(l_ref.shape, F32)
        acc_ref[...] = jnp.zeros(acc_ref.shape, F32)

    def update(s, vmat):
        m_prev = m_ref[...]
        m_new = jnp.maximum(m_prev, jnp.max(s, axis=-1, keepdims=True))
        a = jnp.exp2(m_prev - m_new)
        p = jnp.exp2(s - m_new)
        l_ref[...] = a * l_ref[...] + jnp.sum(p, axis=-1, keepdims=True)
        acc_ref[...] = a * acc_ref[...] + jnp.dot(p.astype(BF16), vmat, preferred_element_type=F32)
        m_ref[...] = m_new

    wqb = wq[...].astype(BF16)
    kcat = jnp.concatenate([r[0].astype(BF16) for r in kp], axis=1)

    def v_page(r):
        heads = [r[0, pl.ds(h, page, stride=DIFF_H), :] for h in range(DIFF_H)]
        return jnp.concatenate(heads, axis=1).astype(BF16)

    vcat = jnp.concatenate([v_page(r) for r in vp], axis=0)
    update(jnp.dot(wqb, kcat, preferred_element_type=F32), vcat)

    @pl.when(s_id == pl.num_programs(1) - 1)
    def _():
        pad = jnp.zeros((16 - TD, 512), F32)
        kn = jnp.concatenate([kn_ref[0], pad], axis=0).astype(BF16)
        vn = jnp.concatenate([vn_ref[0], pad], axis=0).astype(BF16)
        s2 = lax.dot_general(wqb, kn, _NT, preferred_element_type=F32)
        tq = lax.broadcasted_iota(jnp.int32, (R, 16), 0) // 8
        tk = lax.broadcasted_iota(jnp.int32, (R, 16), 1)
        update(jnp.where(tk <= tq, s2, -jnp.inf), vn)

        lam = _lam(lam_ref, lam_init)
        o = acc_ref[...] / l_ref[...]
        hm = lax.broadcasted_iota(jnp.int32, (8, 512), 0)
        lh = lax.broadcasted_iota(jnp.int32, (8, 512), 1) // DIFF_DV
        coef = jnp.where(hm // 2 == lh, jnp.where(hm % 2 == 0, 1.0, -lam), 0.0)
        for t in range(TD):
            d = jnp.sum(o[8 * t:8 * t + 8, :] * coef, axis=0, keepdims=True)
            for h in range(DIFF_H):
                sl = slice(DIFF_DV * h, DIFF_DV * (h + 1))
                o_ref[0, t:t + 1, sl] = _rms(d[:, sl], sub_ref[...]) * (1.0 - lam_init)


def _attn_sample(q3, kn3, vn3, cache_kt3, cache_v3, pt_flat, lam4, subln, lam_init, layer, n_phys,
                 n_pages, P):
    DB, TD, _ = q3.shape
    page = cache_kt3.shape[2]
    base = layer * n_phys

    def page_map(j):
        return lambda b, s, pt: (base + pt[b * n_pages + s * P + j], 0, 0)

    tok = pl.BlockSpec((1, TD, 512), lambda b, s, pt: (b, 0, 0))
    in_specs = [
        tok, tok, tok,
        pl.BlockSpec((4, DIFF_DQK), lambda b, s, pt: (0, 0)),
        pl.BlockSpec((1, DIFF_DV), lambda b, s, pt: (0, 0)),
    ]
    in_specs += [pl.BlockSpec((1, 512, page), page_map(j)) for j in range(P)]
    in_specs += [pl.BlockSpec((1, page * DIFF_H, DIFF_DV), page_map(j)) for j in range(P)]
    R = 8 * TD
    grid_spec = pltpu.PrefetchScalarGridSpec(
        num_scalar_prefetch=1,
        grid=(DB, n_pages // P),
        in_specs=in_specs,
        out_specs=pl.BlockSpec((1, TD, 512), lambda b, s, pt: (b, 0, 0)),
        scratch_shapes=[
            pltpu.VMEM((R, 512), F32),
            pltpu.VMEM((R, 1), F32),
            pltpu.VMEM((R, 1), F32),
            pltpu.VMEM((R, 512), F32),
        ],
    )
    return pl.pallas_call(
        functools.partial(_attn_sample_kernel, P=P, TD=TD, page=page, lam_init=lam_init),
        grid_spec=grid_spec,
        out_shape=jax.ShapeDtypeStruct((DB, TD, 512), F32),
        compiler_params=_params("parallel", "arbitrary"),
    )(pt_flat, q3, kn3, vn3, lam4, subln.reshape(1, DIFF_DV),
      *([cache_kt3] * P), *([cache_v3] * P))


def _ret_sample_kernel(q_ref, k_ref, kT_ref, v_ref, g_ref, gn_ref, S_ref, dec_ref, cross_ref,
                       tail_ref, gl_ref, o_ref, s_out_ref, *, TD):
    q = q_ref[...]
    k = k_ref[...]
    v = v_ref[...]
    S = S_ref[...]
    sc = jnp.einsum('gtd,gsd->gts', q, k, preferred_element_type=F32) * dec_ref[...]
    o = jnp.einsum('gtd,gde->gte', q, S, preferred_element_type=F32) * cross_ref[...]
    for s in range(TD):
        o = o + sc[:, :, s:s + 1] * v[:, s:s + 1, :]
    kT = kT_ref[...]
    vt = v * tail_ref[...]
    S_new = gl_ref[...] * S
    for s in range(TD):
        S_new = S_new + kT[:, :, s:s + 1] * vt[:, s:s + 1, :]
    s_out_ref[...] = S_new
    mu = jnp.mean(o, axis=-1, keepdims=True)
    d = o - mu
    var = jnp.mean(d * d, axis=-1, keepdims=True)
    o_ref[...] = d * lax.rsqrt(var + EPS) * gn_ref[...] * _silu(g_ref[...])


def _ret_sample(q, k, v, g, gn, S_prev, TD, Gb):
    G = q.shape[0]
    lg = jnp.tile(_log_gamma(), G // RET_H)
    decay, cross, tail, gl = _ret_tables(TD)
    dec_g = jnp.tile(decay, (G // RET_H, 1, 1))
    cross_g = jnp.tile(cross.T, (G // RET_H, 1))[:, :, None]
    tail_g = jnp.tile(tail.T, (G // RET_H, 1))[:, :, None]
    gl_g = jnp.tile(gl, G // RET_H)[:, None, None]
    del lg
    kT = jnp.swapaxes(k, 1, 2)
    b3 = lambda shp: pl.BlockSpec((Gb,) + shp, lambda i: (i, 0, 0))
    return pl.pallas_call(
        functools.partial(_ret_sample_kernel, TD=TD),
        grid=(G // Gb,),
        in_specs=[b3((TD, 64)), b3((TD, 64)), b3((64, TD)), b3((TD, 64)), b3((TD, 64)), b3((1, 64)),
                  b3((64, 64)), b3((TD, TD)), b3((TD, 1)), b3((TD, 1)), b3((1, 1))],
        out_specs=[b3((TD, 64)), b3((64, 64))],
        out_shape=[jax.ShapeDtypeStruct((G, TD, 64), F32), jax.ShapeDtypeStruct((G, 64, 64), F32)],
        compiler_params=_params("parallel"),
    )(q, k, kT, v, g, gn, S_prev, dec_g, cross_g, tail_g, gl_g)


def _conv_sample_kernel(ext_ref, w_ref, b_ref, g_ref, bb_ref, o_ref, *, TD):
    acc = jnp.zeros(o_ref.shape, F32)
    for j in range(CONV_K):
        acc = acc + ext_ref[:, j:j + TD, :] * w_ref[j:j + 1, :]
    o_ref[...] = _conv_finish(acc, b_ref[...], g_ref[...], bb_ref[...])


def _conv_sample(ext, w, b, g, bb, TD):
    DB = ext.shape[0]
    vec = lambda a: a.reshape(1, CONV_W)
    return pl.pallas_call(
        functools.partial(_conv_sample_kernel, TD=TD),
        out_shape=jax.ShapeDtypeStruct((DB, TD, CONV_W), F32),
    )(ext, w, vec(b), vec(g), vec(bb))


def _pick(n, pref):
    t = min(n, pref)
    while n % t:
        t //= 2
    return t


def kernel(x_prompt, x_sample, cache_k, cache_v, state_conv, state_ret, page_table, norm_mix, w_in, conv_w, conv_b, conv_ln_g, conv_ln_b, lam_q1, lam_k1, lam_q2, lam_k2, diff_subln, ret_gn, w_out, norm_mlp, w_up, w_down, norm_final):
    B, S, D = x_prompt.shape
    DB, TD, _ = x_sample.shape
    depth = w_in.shape[0]
    n_phys, page = cache_k.shape[1], cache_k.shape[2]
    n_pages = page_table.shape[1]
    past_len = n_pages * page
    Mp, Ms = B * S, DB * TD

    T_attn = _pick(S, 512)
    tm_in = T_attn
    L_ret = _pick(S, 256)
    tc_conv = _pick(S, 512)
    tm_mlp = _pick(Mp, 512)
    tf_mlp = _pick(w_up.shape[2], 1024)
    P = _pick(n_pages, 16)
    Gb = _pick(DB * RET_H, 8)

    w_in_bf = w_in.astype(BF16)
    w_out_bf = w_out.astype(BF16)
    w_up_bf = w_up.astype(BF16)
    w_down_bf = w_down.astype(BF16)

    tab_p = _rope_table(jnp.arange(S))
    tab_s = jnp.tile(_rope_table(past_len + jnp.arange(TD)), (DB, 1))
    lam_all = jnp.stack([lam_q1, lam_k1, lam_q2, lam_k2], axis=1)
    w_vt_bf = jnp.swapaxes(w_in[:, :, _OFF_VD:_OFF_QR], 1, 2).astype(BF16)
    cache_kt3 = jnp.transpose(cache_k, (0, 1, 3, 4, 2)).reshape(
        depth * n_phys, 2 * DIFF_H * DIFF_DQK, page)
    cache_v3 = cache_v.reshape(depth * n_phys, page * DIFF_H, DIFF_DV)
    pt_flat = page_table.reshape(-1)

    xp = x_prompt.reshape(Mp, D)
    xs = x_sample.reshape(Ms, D)
    kp, vp, cp, rp, ks_, vs_, cs_, rs_ = [], [], [], [], [], [], [], []
    for l in range(depth):
        lam_init = 0.8 - 0.6 * math.exp(-0.3 * l)
        last = l == depth - 1

        u, qk, k_d, v_d, vt, ret, g_r = _inproj(xp, norm_mix[l], w_in_bf[l], w_vt_bf[l], tab_p, tm_in)
        o_d = _attn_prompt(qk, vt, lam_all[l], diff_subln[l], lam_init, B, S, T_attn)
        o_r, s_fin = _ret_prompt(ret, g_r, ret_gn[l], B, S, L_ret)
        c = _conv_prompt(u, conv_w[l], conv_b[l], conv_ln_g[l], conv_ln_b[l], B, S, tc_conv)
        xp = _outproj_mlp(xp, c, o_d, o_r, w_out_bf[l], norm_mlp[l], w_up_bf[l], w_down_bf[l],
                          norm_final, last, tm_mlp, tf_mlp)
        kp.append(k_d.reshape(B, S, 2 * DIFF_H, DIFF_DQK))
        vp.append(v_d.reshape(B, S, DIFF_H, DIFF_DV))
        cp.append(u.reshape(B, S, CONV_W)[:, S - (CONV_K - 1):])
        rp.append(s_fin)

        u, qk, k_d, v_d, _, ret, g_r = _inproj(xs, norm_mix[l], w_in_bf[l], w_vt_bf[l], tab_s, Ms)
        q3 = qk[:, 0:512].astype(F32).reshape(DB, TD, 512)
        o_d = _attn_sample(q3, k_d.reshape(DB, TD, 512), v_d.reshape(DB, TD, 512), cache_kt3,
                           cache_v3, pt_flat, lam_all[l], diff_subln[l], lam_init,
                           l, n_phys, n_pages, P)
        to_g = lambda a: a.astype(F32).reshape(DB, TD, RET_H, 64).swapaxes(1, 2).reshape(
            DB * RET_H, TD, 64)
        gn_g = jnp.tile(ret_gn[l].reshape(RET_H, 1, RET_DV), (DB, 1, 1))
        o_r, s_new = _ret_sample(to_g(ret[:, 0:256]), to_g(ret[:, 256:512]), to_g(ret[:, 512:768]),
                                 to_g(g_r), gn_g, state_ret[l].reshape(DB * RET_H, RET_DK, RET_DV),
                                 TD, Gb)
        o_r = o_r.reshape(DB, RET_H, TD, RET_DV).swapaxes(1, 2).reshape(Ms, RET_W)
        ext = jnp.concatenate([state_conv[l], u.reshape(DB, TD, CONV_W)], axis=1)
        c = _conv_sample(ext, conv_w[l], conv_b[l], conv_ln_g[l], conv_ln_b[l], TD)
        xs = _outproj_mlp(xs, c.reshape(Ms, CONV_W), o_d.reshape(Ms, DIFF_W), o_r, w_out_bf[l],
                          norm_mlp[l], w_up_bf[l], w_down_bf[l], norm_final, last, Ms, tf_mlp)
        ks_.append(k_d.reshape(DB, TD, 2 * DIFF_H, DIFF_DQK))
        vs_.append(v_d.reshape(DB, TD, DIFF_H, DIFF_DV))
        cs_.append(ext[:, TD:])
        rs_.append(s_new.reshape(DB, RET_H, RET_DK, RET_DV))

    return (xp.reshape(B, S, D), xs.reshape(DB, TD, D),
            jnp.stack(kp), jnp.stack(vp), jnp.stack(cp), jnp.stack(rp),
            jnp.stack(ks_), jnp.stack(vs_), jnp.stack(cs_), jnp.stack(rs_))
```

```python
import functools
import math

import jax
import jax.numpy as jnp
from jax import lax
from jax.experimental import pallas as pl
from jax.experimental.pallas import tpu as pltpu

F32 = jnp.float32
BF16 = jnp.bfloat16

CONV_W = 256
CONV_K = 31
DIFF_H = 4
DIFF_DQK = 64
DIFF_DV = 128
DIFF_W = DIFF_H * DIFF_DV
RET_H = 4
RET_DK = 64
RET_DV = 64
RET_W = RET_H * RET_DV
ROPE_THETA = 500000.0
ROPE_ROT = DIFF_DQK // 4
RET_THETA = 10000.0
EPS = 1e-5
LANES = 128
CONV_HALO = 32
VMEM_LIMIT = 56 * 1024 * 1024

_OFF_C, _OFF_QD, _OFF_KD, _OFF_VD, _OFF_QR, _OFF_KR, _OFF_VR, _OFF_GR, _D_IN = (
    0, 512, 1024, 1536, 2048, 2304, 2560, 2816, 3072)

_NT = (((1,), (1,)), ((), ()))

Q_SCALE = DIFF_DQK ** -0.5 * math.log2(math.e)


def _params(*sem):
    return pltpu.CompilerParams(dimension_semantics=sem, vmem_limit_bytes=VMEM_LIMIT)


def _rms(x, g):
    return x * lax.rsqrt(jnp.mean(x * x, axis=-1, keepdims=True) + EPS) * g


def _silu(x):
    return x / (1.0 + jnp.exp(-x))


def _lam(lam_ref, lam_init):
    s1 = jnp.sum(lam_ref[0:1, :] * lam_ref[1:2, :], axis=-1, keepdims=True)
    s2 = jnp.sum(lam_ref[2:3, :] * lam_ref[3:4, :], axis=-1, keepdims=True)
    return jnp.exp(s1) - jnp.exp(s2) + lam_init


def _rope_table(pos):
    posf = pos.astype(F32)[:, None]
    n = pos.shape[0]

    def one(n_rot, theta, hd):
        half = n_rot // 2
        inv = 1.0 / (theta ** (jnp.arange(half, dtype=F32) * 2.0 / n_rot))
        ang = posf * inv[None, :]
        cos, sin = jnp.cos(ang), jnp.sin(ang)
        c = jnp.concatenate([cos, cos, jnp.ones((n, hd - n_rot), F32)], axis=-1)
        s = jnp.concatenate([-sin, sin, jnp.zeros((n, hd - n_rot), F32)], axis=-1)
        return jnp.tile(c, (1, LANES // hd)), jnp.tile(s, (1, LANES // hd))

    cd, sd = one(ROPE_ROT, ROPE_THETA, DIFF_DQK)
    cr, sr = one(RET_DK, RET_THETA, RET_DK)
    return jnp.concatenate([cd, sd, cr, sr], axis=-1)


def _log_gamma():
    return jnp.log(1.0 - jnp.exp2(-5.0 - jnp.arange(RET_H, dtype=F32)))


def _ret_tables(L):
    lg = _log_gamma()
    idx = jnp.arange(L, dtype=F32)
    diff = idx[:, None] - idx[None, :]
    decay = jnp.where(diff >= 0, jnp.exp(jnp.maximum(diff, 0.0)[None] * lg[:, None, None]), 0.0)
    cross = jnp.exp((idx[:, None] + 1.0) * lg[None, :])
    tail = jnp.exp((L - 1.0 - idx)[:, None] * lg[None, :])
    gl = jnp.exp(L * lg)
    return decay, cross, tail, gl


def _inproj_kernel(x_ref, g_ref, w_ref, wvt_ref, t_ref, u_ref, qk_ref, k_ref, v_ref, vt_ref,
                   ret_ref, gr_ref):
    h = _rms(x_ref[...], g_ref[...]).astype(BF16)

    def proj(a, b):
        return jnp.dot(h, w_ref[:, a:b], preferred_element_type=F32)

    lane = lax.broadcasted_iota(jnp.int32, (1, LANES), 1) % 64

    def rope(xg, c, s, half):
        partner = jnp.where(lane < half, pltpu.roll(xg, LANES - half, 1), pltpu.roll(xg, half, 1))
        return xg * c + partner * s

    cd, sd = t_ref[:, 0:128], t_ref[:, 128:256]
    cr, sr = t_ref[:, 256:384], t_ref[:, 384:512]

    c_ag = proj(_OFF_C, _OFF_QD)
    u_ref[...] = c_ag[:, :CONV_W] / (1.0 + jnp.exp(-c_ag[:, CONV_W:]))

    q = proj(_OFF_QD, _OFF_KD)
    for j in range(4):
        sl = slice(LANES * j, LANES * (j + 1))
        qk_ref[:, sl] = (rope(q[:, sl], cd, sd, ROPE_ROT // 2) * Q_SCALE).astype(BF16)
    k = proj(_OFF_KD, _OFF_VD)
    for j in range(4):
        sl = slice(LANES * j, LANES * (j + 1))
        kj = rope(k[:, sl], cd, sd, ROPE_ROT // 2)
        k_ref[:, sl] = kj
        qk_ref[:, 512 + LANES * j:512 + LANES * (j + 1)] = kj.astype(BF16)
    v = proj(_OFF_VD, _OFF_QR)
    tm = v.shape[0]
    for j in range(DIFF_H):
        v_ref[pl.ds(j, tm, stride=DIFF_H), :] = v[:, DIFF_DV * j:DIFF_DV * (j + 1)]
    vt_ref[0] = lax.dot_general(wvt_ref[...], h, _NT, preferred_element_type=F32).astype(BF16)

    r = proj(_OFF_QR, _OFF_GR)
    for j in range(2):
        sl = slice(LANES * j, LANES * (j + 1))
        ret_ref[:, sl] = rope(r[:, sl], cr, sr, RET_DK // 2).astype(BF16)
    for j in range(2, 4):
        sl = slice(LANES * j, LANES * (j + 1))
        ret_ref[:, sl] = (rope(r[:, sl], cr, sr, RET_DK // 2) * (RET_DK ** -0.5)).astype(BF16)
    ret_ref[:, 512:768] = r[:, 512:768].astype(BF16)
    gr_ref[...] = proj(_OFF_GR, _D_IN)


def _inproj(x2d, g, w_bf, wvt_bf, layer, table, tm):
    M, D = x2d.shape
    n_t = table.shape[0] // tm
    row = lambda i: (i, 0)
    return pl.pallas_call(
        _inproj_kernel,
        grid=(M // tm,),
        in_specs=[
            pl.BlockSpec((tm, D), row),
            pl.BlockSpec((1, D), lambda i: (0, 0)),
            pl.BlockSpec((None, D, _D_IN), lambda i: (layer, 0, 0)),
            pl.BlockSpec((None, DIFF_W, D), lambda i: (layer, 0, 0)),
            pl.BlockSpec((tm, 512), lambda i: (i % n_t, 0)),
        ],
        out_specs=[
            pl.BlockSpec((tm, CONV_W), row),
            pl.BlockSpec((tm, 1024), row),
            pl.BlockSpec((tm, 512), row),
            pl.BlockSpec((tm * DIFF_H, DIFF_DV), row),
            pl.BlockSpec((1, DIFF_W, tm), lambda i: (i, 0, 0)),
            pl.BlockSpec((tm, 768), row),
            pl.BlockSpec((tm, RET_W), row),
        ],
        out_shape=[
            jax.ShapeDtypeStruct((M, CONV_W), F32),
            jax.ShapeDtypeStruct((M, 1024), BF16),
            jax.ShapeDtypeStruct((M, 512), F32),
            jax.ShapeDtypeStruct((M * DIFF_H, DIFF_DV), F32),
            jax.ShapeDtypeStruct((M // tm, DIFF_W, tm), BF16),
            jax.ShapeDtypeStruct((M, 768), BF16),
            jax.ShapeDtypeStruct((M, RET_W), F32),
        ],
        compiler_params=_params("parallel"),
    )(x2d, g.reshape(1, D), w_bf, wvt_bf, table)


def _attn_prompt_kernel(q_ref, k_ref, vt_ref, lam_ref, sub_ref, o_ref,
                        qs, sa_ref, sb_ref, m_ref, l_ref, acc_ref, *, T, NC, nq, lam_init):
    C = T // NC
    S = nq * T
    q = q_ref[...]
    lane = lax.broadcasted_iota(jnp.int32, (S, LANES), 1)
    qs[0] = jnp.where(lane < DIFF_DQK, q, jnp.zeros_like(q))
    qs[1] = jnp.where(lane >= DIFF_DQK, q, jnp.zeros_like(q))
    lam = _lam(lam_ref, lam_init)
    chains = [(mp, c) for mp in range(2) for c in range(NC)]

    def scores(qi, ki, s_ref):
        k = k_ref[ki * T:(ki + 1) * T, :]
        for mp, c in chains:
            cols = slice(c * C, (c + 1) * C)
            s_ref[mp, :, cols] = lax.dot_general(
                k, qs[mp, qi * T + c * C:qi * T + (c + 1) * C, :], _NT, preferred_element_type=F32)

    def accumulate(qi, ki, s_ref):
        diagonal = ki == qi
        for mp, c in chains:
            cols = slice(c * C, (c + 1) * C)
            nk = (c + 1) * C if diagonal else T
            s = s_ref[mp, 0:nk, cols]
            if diagonal:
                key = lax.broadcasted_iota(jnp.int32, (nk, C), 0)
                qq = lax.broadcasted_iota(jnp.int32, (nk, C), 1) + c * C
                s = jnp.where(key <= qq, s, -jnp.inf)
            pv = lambda p: jnp.dot(vt_ref[ki, :, 0:nk], p.astype(BF16),
                                   preferred_element_type=F32)
            if ki == 0:
                m_new = jnp.max(s, axis=0, keepdims=True)
                p = jnp.exp2(s - m_new)
                l_ref[mp, :, cols] = jnp.sum(p, axis=0, keepdims=True)
                acc_ref[mp, :, cols] = pv(p)
            else:
                m_prev = m_ref[mp, :, cols]
                m_new = jnp.maximum(m_prev, jnp.max(s, axis=0, keepdims=True))
                a = jnp.exp2(m_prev - m_new)
                p = jnp.exp2(s - m_new)
                l_ref[mp, :, cols] = a * l_ref[mp, :, cols] + jnp.sum(p, axis=0, keepdims=True)
                acc_ref[mp, :, cols] = a * acc_ref[mp, :, cols] + pv(p)
            m_ref[mp, :, cols] = m_new

    def finish(qi):
        d = acc_ref[0] / l_ref[0] - lam * (acc_ref[1] / l_ref[1])
        r = lax.rsqrt(jnp.mean(d * d, axis=0, keepdims=True) + EPS)
        y = d * r * (sub_ref[...] * (1.0 - lam_init))
        o_ref[qi * T:(qi + 1) * T, :] = y.T.astype(BF16)

    pairs = [(qi, ki) for qi in range(nq) for ki in range(qi + 1)]
    bufs = (sa_ref, sb_ref)
    scores(*pairs[0], bufs[0])
    for t, (qi, ki) in enumerate(pairs):
        if t + 1 < len(pairs):
            scores(*pairs[t + 1], bufs[(t + 1) % 2])
        accumulate(qi, ki, bufs[t % 2])
        if ki == qi:
            finish(qi)


def _attn_prompt(qk, vt, lam4, subln, lam_init, B, S, T, NC=2):
    M = qk.shape[0]
    nq = S // T
    return pl.pallas_call(
        functools.partial(_attn_prompt_kernel, T=T, NC=NC, nq=nq, lam_init=lam_init),
        grid=(B, DIFF_H),
        in_specs=[
            pl.BlockSpec((S, LANES), lambda b, h: (b, h)),
            pl.BlockSpec((S, LANES), lambda b, h: (b, DIFF_H + h)),
            pl.BlockSpec((nq, DIFF_DV, T), lambda b, h: (b, h, 0)),
            pl.BlockSpec((4, DIFF_DQK), lambda b, h: (0, 0)),
            pl.BlockSpec((DIFF_DV, 1), lambda b, h: (0, 0)),
        ],
        out_specs=pl.BlockSpec((S, LANES), lambda b, h: (b, h)),
        out_shape=jax.ShapeDtypeStruct((M, DIFF_W), BF16),
        scratch_shapes=[
            pltpu.VMEM((2, S, LANES), BF16),
            pltpu.VMEM((2, T, T), F32),
            pltpu.VMEM((2, T, T), F32),
            pltpu.VMEM((2, 1, T), F32),
            pltpu.VMEM((2, 1, T), F32),
            pltpu.VMEM((2, DIFF_DV, T), F32),
        ],
        compiler_params=_params("parallel", "parallel"),
    )(qk, qk, vt, lam4, subln.reshape(DIFF_DV, 1))


def _group_norm_gate(o, g, gn):
    hid = lax.broadcasted_iota(jnp.int32, (1, RET_W), 1) // RET_DV

    def group_mean(x):
        out = jnp.zeros_like(x)
        for h in range(RET_H):
            mh = hid == h
            s = jnp.sum(jnp.where(mh, x, 0.0), axis=-1, keepdims=True) * (1.0 / RET_DV)
            out = jnp.where(mh, s, out)
        return out

    d = o - group_mean(o)
    var = group_mean(d * d)
    return d * lax.rsqrt(var + EPS) * gn * _silu(g)


def _ret_prompt_kernel(q_ref, k_ref, v_ref, g_ref, gn_ref, dec_ref, cross_ref, tail_ref, gl_ref,
                       o_ref, s_out_ref, S_ref):
    c = pl.program_id(1)

    @pl.when(c == 0)
    def _():
        S_ref[...] = jnp.zeros(S_ref.shape, F32)

    q = q_ref[...]
    k = k_ref[...]
    v = v_ref[...]
    hid = lax.broadcasted_iota(jnp.int32, (1, RET_W), 1) // RET_DV
    Sb = S_ref[...]
    o = jnp.dot(q, Sb.astype(BF16), preferred_element_type=F32) * cross_ref[...]
    for h in range(RET_H):
        mh = hid == h
        qh = jnp.where(mh, q, jnp.zeros_like(q))
        sc = lax.dot_general(qh, k, _NT, preferred_element_type=F32) * dec_ref[h]
        oh = jnp.dot(sc.astype(BF16), v, preferred_element_type=F32)
        o = o + jnp.where(mh, oh, 0.0)
    kt = (k.astype(F32) * tail_ref[...]).astype(BF16)
    upd = lax.dot_general(kt, v, (((0,), (0,)), ((), ())), preferred_element_type=F32)
    rowh = lax.broadcasted_iota(jnp.int32, (RET_W, RET_W), 0) // RET_DK
    colh = lax.broadcasted_iota(jnp.int32, (RET_W, RET_W), 1) // RET_DV
    S_new = gl_ref[...] * Sb + jnp.where(rowh == colh, upd, 0.0)
    S_ref[...] = S_new
    o_ref[...] = _group_norm_gate(o, g_ref[...], gn_ref[...]).astype(BF16)

    @pl.when(c == pl.num_programs(1) - 1)
    def _():
        y = S_new
        for sft in (64, 128, 192):
            y = y + pltpu.roll(S_new, sft, 1)
        s_out_ref[0] = y[:, 0:RET_DV]


def _ret_prompt(ret, gr, gn, B, S, L):
    M = ret.shape[0]
    nc = S // L
    decay, cross, tail, gl = _ret_tables(L)
    rep = lambda a: jnp.repeat(a, RET_DV, axis=-1)
    row = lambda b, c: (b * nc + c, 0)
    const2 = lambda b, c: (0, 0)
    o_r, s_fin = pl.pallas_call(
        _ret_prompt_kernel,
        grid=(B, nc),
        in_specs=[
            pl.BlockSpec((L, RET_W), lambda b, c: (b * nc + c, 0)),
            pl.BlockSpec((L, RET_W), lambda b, c: (b * nc + c, 1)),
            pl.BlockSpec((L, RET_W), lambda b, c: (b * nc + c, 2)),
            pl.BlockSpec((L, RET_W), row),
            pl.BlockSpec((1, RET_W), const2),
            pl.BlockSpec((RET_H, L, L), lambda b, c: (0, 0, 0)),
            pl.BlockSpec((L, RET_W), const2),
            pl.BlockSpec((L, RET_W), const2),
            pl.BlockSpec((1, RET_W), const2),
        ],
        out_specs=[
            pl.BlockSpec((L, RET_W), row),
            pl.BlockSpec((1, RET_W, RET_DV), lambda b, c: (b, 0, 0)),
        ],
        out_shape=[
            jax.ShapeDtypeStruct((M, RET_W), BF16),
            jax.ShapeDtypeStruct((B, RET_W, RET_DV), F32),
        ],
        scratch_shapes=[pltpu.VMEM((RET_W, RET_W), F32)],
        compiler_params=_params("parallel", "arbitrary"),
    )(ret, ret, ret, gr, gn.reshape(1, RET_W), decay, rep(cross), rep(tail), rep(gl[None, :]))
    return o_r, s_fin.reshape(B, RET_H, RET_DK, RET_DV)


def _conv_finish(acc, b, g, bb):
    y = acc + b
    mu = jnp.mean(y, axis=-1, keepdims=True)
    d = y - mu
    var = jnp.mean(d * d, axis=-1, keepdims=True)
    return _silu(d * lax.rsqrt(var + EPS) * g + bb)


def _conv_prompt_kernel(prev_ref, cur_ref, w_ref, b_ref, g_ref, bb_ref, o_ref, buf, sh, *, tc, sub):
    i = pl.program_id(1)
    prev = prev_ref[...]
    buf[0:CONV_HALO, :] = jnp.where(i == 0, jnp.zeros_like(prev), prev)
    buf[CONV_HALO:CONV_HALO + tc, :] = cur_ref[...]
    off = CONV_HALO - (CONV_K - 1)
    cls = [[j for j in range(CONV_K) if (off + j) % 8 == a] for a in range(8)]
    for a, taps in enumerate(cls):
        n = tc + taps[-1] - taps[0]
        sh[a, 0:n, :] = buf[off + taps[0]:off + taps[0] + n, :]

    for r0 in range(0, tc, sub):
        acc = jnp.zeros((sub, CONV_W), F32)
        for a, taps in enumerate(cls):
            for j in taps:
                acc = acc + sh[a, r0 + j - taps[0]:r0 + j - taps[0] + sub, :] * w_ref[j:j + 1, :]
        o_ref[r0:r0 + sub, :] = _conv_finish(acc, b_ref[...], g_ref[...], bb_ref[...]).astype(BF16)


def _conv_prompt(u, w, b, g, bb, B, S, tc, sub=64):
    M = u.shape[0]
    nb = S // tc
    hb = tc // CONV_HALO
    vec = lambda a: a.reshape(1, CONV_W)
    const2 = lambda bi, i: (0, 0)
    return pl.pallas_call(
        functools.partial(_conv_prompt_kernel, tc=tc, sub=sub),
        grid=(B, nb),
        in_specs=[
            pl.BlockSpec((CONV_HALO, CONV_W),
                         lambda bi, i: (jnp.maximum((bi * nb + i) * hb - 1, 0), 0)),
            pl.BlockSpec((tc, CONV_W), lambda bi, i: (bi * nb + i, 0)),
            pl.BlockSpec((CONV_HALO, CONV_W), const2),
            pl.BlockSpec((1, CONV_W), const2),
            pl.BlockSpec((1, CONV_W), const2),
            pl.BlockSpec((1, CONV_W), const2),
        ],
        out_specs=pl.BlockSpec((tc, CONV_W), lambda bi, i: (bi * nb + i, 0)),
        out_shape=jax.ShapeDtypeStruct((M, CONV_W), BF16),
        scratch_shapes=[pltpu.VMEM((CONV_HALO + tc, CONV_W), F32),
                        pltpu.VMEM((8, tc + CONV_HALO - 8, CONV_W), F32)],
        compiler_params=_params("parallel", "arbitrary"),
    )(u, u, jnp.pad(w, ((0, CONV_HALO - CONV_K), (0, 0))), vec(b), vec(g), vec(bb))


def _outproj_mlp_kernel(x_ref, c_ref, od_ref, or_ref, wo_ref, g_ref, wu_ref, wd_ref, gf_ref,
                        o_ref, a_ref, *, final_norm, tf):
    mix = jnp.dot(c_ref[...].astype(BF16), wo_ref[0:CONV_W, :], preferred_element_type=F32)
    mix += jnp.dot(od_ref[...].astype(BF16), wo_ref[CONV_W:CONV_W + DIFF_W, :],
                   preferred_element_type=F32)
    mix += jnp.dot(or_ref[...].astype(BF16), wo_ref[CONV_W + DIFF_W:, :],
                   preferred_element_type=F32)
    xm = x_ref[...] + mix
    h2 = _rms(xm, g_ref[...]).astype(BF16)
    for f0 in range(0, wu_ref.shape[1], tf):
        up = jnp.dot(h2, wu_ref[:, f0:f0 + tf], preferred_element_type=F32)
        a_ref[:, f0:f0 + tf] = jnp.square(jnp.maximum(up, 0.0)).astype(BF16)
    y = xm + jnp.dot(a_ref[...], wd_ref[...], preferred_element_type=F32)
    if final_norm:
        y = _rms(y, gf_ref[...])
    o_ref[...] = y


def _outproj_mlp(x2d, c, od, orr, wo_bf, g, wu_bf, wd_bf, gf, layer, final_norm, tm, tf):
    M, D = x2d.shape
    FF = wu_bf.shape[2]
    row = lambda i: (i, 0)
    resident = lambda shape: pl.BlockSpec(shape, lambda i: (0, 0), pipeline_mode=pl.Buffered(1))
    weight = lambda shape: pl.BlockSpec((None,) + shape, lambda i: (layer, 0, 0),
                                        pipeline_mode=pl.Buffered(1))
    return pl.pallas_call(
        functools.partial(_outproj_mlp_kernel, final_norm=final_norm, tf=tf),
        grid=(M // tm,),
        in_specs=[
            pl.BlockSpec((tm, D), row),
            pl.BlockSpec((tm, CONV_W), row),
            pl.BlockSpec((tm, DIFF_W), row),
            pl.BlockSpec((tm, RET_W), row),
            weight((D, D)),
            resident((1, D)),
            weight((D, FF)),
            weight((FF, D)),
            resident((1, D)),
        ],
        out_specs=pl.BlockSpec((tm, D), row),
        out_shape=jax.ShapeDtypeStruct((M, D), F32),
        scratch_shapes=[pltpu.VMEM((tm, FF), BF16)],
        compiler_params=_params("parallel"),
    )(x2d, c, od, orr, wo_bf, g.reshape(1, D), wu_bf, wd_bf, gf.reshape(1, D))


def _attn_sample_kernel(pt_ref, q_ref, kn_ref, vn_ref, lam_ref, sub_ref, *rest,
                        P, TD, page, lam_init):
    kp = rest[0:P]
    vp = rest[P:2 * P]
    o_ref, wq, m_ref, l_ref, acc_ref = rest[2 * P:]
    R = 8 * TD
    s_id = pl.program_id(1)

    @pl.when(s_id == 0)
    def _():
        q = q_ref[0]
        hm = lax.broadcasted_iota(jnp.int32, (8, 512), 0)
        lh = lax.broadcasted_iota(jnp.int32, (8, 512), 1) // DIFF_DQK
        for t in range(TD):
            wq[8 * t:8 * t + 8, :] = jnp.where(hm == lh, jnp.broadcast_to(q[t:t + 1, :], (8, 512)), 0.0)
        m_ref[...] = jnp.full(m_ref.shape, -jnp.inf, F32)
        l_ref[...] = jnp.zeros(l_ref.shape, F32)
        acc_ref[...] = jnp.zeros(acc_ref.shape, F32)

    def update(s, vmat):
        m_prev = m_ref[...]
        m_new = jnp.maximum(m_prev, jnp.max(s, axis=-1, keepdims=True))
        a = jnp.exp2(m_prev - m_new)
        p = jnp.exp2(s - m_new)
        l_ref[...] = a * l_ref[...] + jnp.sum(p, axis=-1, keepdims=True)
        acc_ref[...] = a * acc_ref[...] + jnp.dot(p.astype(BF16), vmat, preferred_element_type=F32)
        m_ref[...] = m_new

    wqb = wq[...].astype(BF16)
    kcat = jnp.concatenate([r[0].astype(BF16) for r in kp], axis=1)

    def v_page(r):
        heads = [r[0, pl.ds(h, page, stride=DIFF_H), :] for h in range(DIFF_H)]
        return jnp.concatenate(heads, axis=1).astype(BF16)

    vcat = jnp.concatenate([v_page(r) for r in vp], axis=0)
    update(jnp.dot(wqb, kcat, preferred_element_type=F32), vcat)

    @pl.when(s_id == pl.num_programs(1) - 1)
    def _():
        pad = jnp.zeros((16 - TD, 512), F32)
        kn = jnp.concatenate([kn_ref[0], pad], axis=0).astype(BF16)
        vn = jnp.concatenate([vn_ref[0], pad], axis=0).astype(BF16)
        s2 = lax.dot_general(wqb, kn, _NT, preferred_element_type=F32)
        tq = lax.broadcasted_iota(jnp.int32, (R, 16), 0) // 8
        tk = lax.broadcasted_iota(jnp.int32, (R, 16), 1)
        update(jnp.where(tk <= tq, s2, -jnp.inf), vn)

        lam = _lam(lam_ref, lam_init)
        o = acc_ref[...] / l_ref[...]
        hm = lax.broadcasted_iota(jnp.int32, (8, 512), 0)
        lh = lax.broadcasted_iota(jnp.int32, (8, 512), 1) // DIFF_DV
        coef = jnp.where(hm // 2 == lh, jnp.where(hm % 2 == 0, 1.0, -lam), 0.0)
        for t in range(TD):
            d = jnp.sum(o[8 * t:8 * t + 8, :] * coef, axis=0, keepdims=True)
            for h in range(DIFF_H):
                sl = slice(DIFF_DV * h, DIFF_DV * (h + 1))
                o_ref[0, t:t + 1, sl] = _rms(d[:, sl], sub_ref[...]) * (1.0 - lam_init)


def _attn_sample(q3, kn3, vn3, cache_kt3, cache_v3, pt_flat, lam4, subln, lam_init, layer, n_phys,
                 n_pages, P):
    DB, TD, _ = q3.shape
    page = cache_kt3.shape[2]
    base = layer * n_phys

    def page_map(j):
        return lambda b, s, pt: (base + pt[b * n_pages + s * P + j], 0, 0)

    tok = pl.BlockSpec((1, TD, 512), lambda b, s, pt: (b, 0, 0))
    in_specs = [
        tok, tok, tok,
        pl.BlockSpec((4, DIFF_DQK), lambda b, s, pt: (0, 0)),
        pl.BlockSpec((1, DIFF_DV), lambda b, s, pt: (0, 0)),
    ]
    in_specs += [pl.BlockSpec((1, 512, page), page_map(j)) for j in range(P)]
    in_specs += [pl.BlockSpec((1, page * DIFF_H, DIFF_DV), page_map(j)) for j in range(P)]
    R = 8 * TD
    grid_spec = pltpu.PrefetchScalarGridSpec(
        num_scalar_prefetch=1,
        grid=(DB, n_pages // P),
        in_specs=in_specs,
        out_specs=pl.BlockSpec((1, TD, 512), lambda b, s, pt: (b, 0, 0)),
        scratch_shapes=[
            pltpu.VMEM((R, 512), F32),
            pltpu.VMEM((R, 1), F32),
            pltpu.VMEM((R, 1), F32),
            pltpu.VMEM((R, 512), F32),
        ],
    )
    return pl.pallas_call(
        functools.partial(_attn_sample_kernel, P=P, TD=TD, page=page, lam_init=lam_init),
        grid_spec=grid_spec,
        out_shape=jax.ShapeDtypeStruct((DB, TD, 512), F32),
        compiler_params=_params("parallel", "arbitrary"),
    )(pt_flat, q3, kn3, vn3, lam4, subln.reshape(1, DIFF_DV),
      *([cache_kt3] * P), *([cache_v3] * P))


def _ret_sample_kernel(q_ref, k_ref, kT_ref, v_ref, g_ref, gn_ref, S_ref, dec_ref, cross_ref,
                       tail_ref, gl_ref, o_ref, s_out_ref, *, TD):
    q = q_ref[...]
    k = k_ref[...]
    v = v_ref[...]
    S = S_ref[...]
    sc = jnp.einsum('gtd,gsd->gts', q, k, preferred_element_type=F32) * dec_ref[...]
    o = jnp.einsum('gtd,gde->gte', q, S, preferred_element_type=F32) * cross_ref[...]
    for s in range(TD):
        o = o + sc[:, :, s:s + 1] * v[:, s:s + 1, :]
    kT = kT_ref[...]
    vt = v * tail_ref[...]
    S_new = gl_ref[...] * S
    for s in range(TD):
        S_new = S_new + kT[:, :, s:s + 1] * vt[:, s:s + 1, :]
    s_out_ref[...] = S_new
    mu = jnp.mean(o, axis=-1, keepdims=True)
    d = o - mu
    var = jnp.mean(d * d, axis=-1, keepdims=True)
    o_ref[...] = d * lax.rsqrt(var + EPS) * gn_ref[...] * _silu(g_ref[...])


def _ret_sample(q, k, v, g, gn, S_prev, TD, Gb):
    G = q.shape[0]
    lg = jnp.tile(_log_gamma(), G // RET_H)
    decay, cross, tail, gl = _ret_tables(TD)
    dec_g = jnp.tile(decay, (G // RET_H, 1, 1))
    cross_g = jnp.tile(cross.T, (G // RET_H, 1))[:, :, None]
    tail_g = jnp.tile(tail.T, (G // RET_H, 1))[:, :, None]
    gl_g = jnp.tile(gl, G // RET_H)[:, None, None]
    del lg
    kT = jnp.swapaxes(k, 1, 2)
    b3 = lambda shp: pl.BlockSpec((Gb,) + shp, lambda i: (i, 0, 0))
    return pl.pallas_call(
        functools.partial(_ret_sample_kernel, TD=TD),
        grid=(G // Gb,),
        in_specs=[b3((TD, 64)), b3((TD, 64)), b3((64, TD)), b3((TD, 64)), b3((TD, 64)), b3((1, 64)),
                  b3((64, 64)), b3((TD, TD)), b3((TD, 1)), b3((TD, 1)), b3((1, 1))],
        out_specs=[b3((TD, 64)), b3((64, 64))],
        out_shape=[jax.ShapeDtypeStruct((G, TD, 64), F32), jax.ShapeDtypeStruct((G, 64, 64), F32)],
        compiler_params=_params("parallel"),
    )(q, k, kT, v, g, gn, S_prev, dec_g, cross_g, tail_g, gl_g)


def _conv_sample_kernel(ext_ref, w_ref, b_ref, g_ref, bb_ref, o_ref, *, TD):
    acc = jnp.zeros(o_ref.shape, F32)
    for j in range(CONV_K):
        acc = acc + ext_ref[:, j:j + TD, :] * w_ref[j:j + 1, :]
    o_ref[...] = _conv_finish(acc, b_ref[...], g_ref[...], bb_ref[...])


def _conv_sample(ext, w, b, g, bb, TD):
    DB = ext.shape[0]
    vec = lambda a: a.reshape(1, CONV_W)
    return pl.pallas_call(
        functools.partial(_conv_sample_kernel, TD=TD),
        out_shape=jax.ShapeDtypeStruct((DB, TD, CONV_W), F32),
    )(ext, w, vec(b), vec(g), vec(bb))


def _pick(n, pref):
    t = min(n, pref)
    while n % t:
        t //= 2
    return t


def kernel(x_prompt, x_sample, cache_k, cache_v, state_conv, state_ret, page_table, norm_mix, w_in, conv_w, conv_b, conv_ln_g, conv_ln_b, lam_q1, lam_k1, lam_q2, lam_k2, diff_subln, ret_gn, w_out, norm_mlp, w_up, w_down, norm_final):
    B, S, D = x_prompt.shape
    DB, TD, _ = x_sample.shape
    depth = w_in.shape[0]
    n_phys, page = cache_k.shape[1], cache_k.shape[2]
    n_pages = page_table.shape[1]
    past_len = n_pages * page
    Mp, Ms = B * S, DB * TD

    T_attn = _pick(S, 512)
    tm_in = T_attn
    L_ret = _pick(S, 256)
    tc_conv = _pick(S, 512)
    tm_mlp = _pick(Mp, 512)
    tf_mlp = _pick(w_up.shape[2], 1024)
    P = _pick(n_pages, 16)
    Gb = _pick(DB * RET_H, 8)

    w_in_bf = w_in.astype(BF16)
    w_out_bf = w_out.astype(BF16)
    w_up_bf = w_up.astype(BF16)
    w_down_bf = w_down.astype(BF16)

    tab_p = _rope_table(jnp.arange(S))
    tab_s = jnp.tile(_rope_table(past_len + jnp.arange(TD)), (DB, 1))
    lam_all = jnp.stack([lam_q1, lam_k1, lam_q2, lam_k2], axis=1)
    w_vt_bf = jnp.swapaxes(w_in[:, :, _OFF_VD:_OFF_QR], 1, 2).astype(BF16)
    cache_kt3 = jnp.transpose(cache_k, (0, 1, 3, 4, 2)).reshape(
        depth * n_phys, 2 * DIFF_H * DIFF_DQK, page)
    cache_v3 = cache_v.reshape(depth * n_phys, page * DIFF_H, DIFF_DV)
    pt_flat = page_table.reshape(-1)

    xp = x_prompt.reshape(Mp, D)
    xs = x_sample.reshape(Ms, D)
    kp, vp, cp, rp, ks_, vs_, cs_, rs_ = [], [], [], [], [], [], [], []
    for l in range(depth):
        lam_init = 0.8 - 0.6 * math.exp(-0.3 * l)
        last = l == depth - 1

        u, qk, k_d, v_d, vt, ret, g_r = _inproj(xp, norm_mix[l], w_in_bf, w_vt_bf, l, tab_p, tm_in)
        o_d = _attn_prompt(qk, vt, lam_all[l], diff_subln[l], lam_init, B, S, T_attn)
        o_r, s_fin = _ret_prompt(ret, g_r, ret_gn[l], B, S, L_ret)
        c = _conv_prompt(u, conv_w[l], conv_b[l], conv_ln_g[l], conv_ln_b[l], B, S, tc_conv)
        xp = _outproj_mlp(xp, c, o_d, o_r, w_out_bf, norm_mlp[l], w_up_bf, w_down_bf,
                          norm_final, l, last, tm_mlp, tf_mlp)
        kp.append(k_d.reshape(B, S, 2 * DIFF_H, DIFF_DQK))
        vp.append(v_d.reshape(B, S, DIFF_H, DIFF_DV))
        cp.append(u.reshape(B, S, CONV_W)[:, S - (CONV_K - 1):])
        rp.append(s_fin)

        u, qk, k_d, v_d, _, ret, g_r = _inproj(xs, norm_mix[l], w_in_bf, w_vt_bf, l, tab_s, Ms)
        q3 = qk[:, 0:512].astype(F32).reshape(DB, TD, 512)
        o_d = _attn_sample(q3, k_d.reshape(DB, TD, 512), v_d.reshape(DB, TD, 512), cache_kt3,
                           cache_v3, pt_flat, lam_all[l], diff_subln[l], lam_init,
                           l, n_phys, n_pages, P)
        to_g = lambda a: a.astype(F32).reshape(DB, TD, RET_H, 64).swapaxes(1, 2).reshape(
            DB * RET_H, TD, 64)
        gn_g = jnp.tile(ret_gn[l].reshape(RET_H, 1, RET_DV), (DB, 1, 1))
        o_r, s_new = _ret_sample(to_g(ret[:, 0:256]), to_g(ret[:, 256:512]), to_g(ret[:, 512:768]),
                                 to_g(g_r), gn_g, state_ret[l].reshape(DB * RET_H, RET_DK, RET_DV),
                                 TD, Gb)
        o_r = o_r.reshape(DB, RET_H, TD, RET_DV).swapaxes(1, 2).reshape(Ms, RET_W)
        ext = jnp.concatenate([state_conv[l], u.reshape(DB, TD, CONV_W)], axis=1)
        c = _conv_sample(ext, conv_w[l], conv_b[l], conv_ln_g[l], conv_ln_b[l], TD)
        xs = _outproj_mlp(xs, c.reshape(Ms, CONV_W), o_d.reshape(Ms, DIFF_W), o_r, w_out_bf,
                          norm_mlp[l], w_up_bf, w_down_bf, norm_final, l, last, Ms, tf_mlp)
        ks_.append(k_d.reshape(DB, TD, 2 * DIFF_H, DIFF_DQK))
        vs_.append(v_d.reshape(DB, TD, DIFF_H, DIFF_DV))
        cs_.append(ext[:, TD:])
        rs_.append(s_new.reshape(DB, RET_H, RET_DK, RET_DV))

    return (xp.reshape(B, S, D), xs.reshape(DB, TD, D),
            jnp.stack(kp), jnp.stack(vp), jnp.stack(cp), jnp.stack(rp),
            jnp.stack(ks_), jnp.stack(vs_), jnp.stack(cs_), jnp.stack(rs_))
```

```python
import functools
import math

import jax
import jax.numpy as jnp
from jax import lax
from jax.experimental import pallas as pl
from jax.experimental.pallas import tpu as pltpu

F32 = jnp.float32
BF16 = jnp.bfloat16

CONV_W = 256
CONV_K = 31
DIFF_H = 4
DIFF_DQK = 64
DIFF_DV = 128
DIFF_W = DIFF_H * DIFF_DV
RET_H = 4
RET_DK = 64
RET_DV = 64
RET_W = RET_H * RET_DV
ROPE_THETA = 500000.0
ROPE_ROT = DIFF_DQK // 4
RET_THETA = 10000.0
EPS = 1e-5
LANES = 128
CONV_HALO = 32
VMEM_LIMIT = 56 * 1024 * 1024

_OFF_C, _OFF_QD, _OFF_KD, _OFF_VD, _OFF_QR, _OFF_KR, _OFF_VR, _OFF_GR, _D_IN = (
    0, 512, 1024, 1536, 2048, 2304, 2560, 2816, 3072)

_NT = (((1,), (1,)), ((), ()))

Q_SCALE = DIFF_DQK ** -0.5 * math.log2(math.e)


def _params(*sem):
    return pltpu.CompilerParams(dimension_semantics=sem, vmem_limit_bytes=VMEM_LIMIT)


def _rms(x, g):
    return x * lax.rsqrt(jnp.mean(x * x, axis=-1, keepdims=True) + EPS) * g


def _silu(x):
    return x / (1.0 + jnp.exp(-x))


def _lam(lam_ref, lam_init):
    s1 = jnp.sum(lam_ref[0:1, :] * lam_ref[1:2, :], axis=-1, keepdims=True)
    s2 = jnp.sum(lam_ref[2:3, :] * lam_ref[3:4, :], axis=-1, keepdims=True)
    return jnp.exp(s1) - jnp.exp(s2) + lam_init


def _rope_table(pos):
    posf = pos.astype(F32)[:, None]
    n = pos.shape[0]

    def one(n_rot, theta, hd):
        half = n_rot // 2
        inv = 1.0 / (theta ** (jnp.arange(half, dtype=F32) * 2.0 / n_rot))
        ang = posf * inv[None, :]
        cos, sin = jnp.cos(ang), jnp.sin(ang)
        c = jnp.concatenate([cos, cos, jnp.ones((n, hd - n_rot), F32)], axis=-1)
        s = jnp.concatenate([-sin, sin, jnp.zeros((n, hd - n_rot), F32)], axis=-1)
        return jnp.tile(c, (1, LANES // hd)), jnp.tile(s, (1, LANES // hd))

    cd, sd = one(ROPE_ROT, ROPE_THETA, DIFF_DQK)
    cr, sr = one(RET_DK, RET_THETA, RET_DK)
    return jnp.concatenate([cd, sd, cr, sr], axis=-1)


def _log_gamma():
    return jnp.log(1.0 - jnp.exp2(-5.0 - jnp.arange(RET_H, dtype=F32)))


def _ret_tables(L):
    lg = _log_gamma()
    idx = jnp.arange(L, dtype=F32)
    diff = idx[:, None] - idx[None, :]
    decay = jnp.where(diff >= 0, jnp.exp(jnp.maximum(diff, 0.0)[None] * lg[:, None, None]), 0.0)
    cross = jnp.exp((idx[:, None] + 1.0) * lg[None, :])
    tail = jnp.exp((L - 1.0 - idx)[:, None] * lg[None, :])
    gl = jnp.exp(L * lg)
    return decay, cross, tail, gl


def _inproj_kernel(x_ref, g_ref, w_ref, t_ref, u_ref, qk_ref, k_ref, v_ref, vt_ref,
                   ret_ref, gr_ref):
    h = _rms(x_ref[...], g_ref[...]).astype(BF16)

    def proj(a, b):
        return jnp.dot(h, w_ref[:, a:b], preferred_element_type=F32)

    lane = lax.broadcasted_iota(jnp.int32, (1, LANES), 1) % 64

    def rope(xg, c, s, half):
        partner = jnp.where(lane < half, pltpu.roll(xg, LANES - half, 1), pltpu.roll(xg, half, 1))
        return xg * c + partner * s

    cd, sd = t_ref[:, 0:128], t_ref[:, 128:256]
    cr, sr = t_ref[:, 256:384], t_ref[:, 384:512]

    c_ag = proj(_OFF_C, _OFF_QD)
    u_ref[...] = c_ag[:, :CONV_W] / (1.0 + jnp.exp(-c_ag[:, CONV_W:]))

    q = proj(_OFF_QD, _OFF_KD)
    for j in range(4):
        sl = slice(LANES * j, LANES * (j + 1))
        qk_ref[:, sl] = (rope(q[:, sl], cd, sd, ROPE_ROT // 2) * Q_SCALE).astype(BF16)
    k = proj(_OFF_KD, _OFF_VD)
    for j in range(4):
        sl = slice(LANES * j, LANES * (j + 1))
        kj = rope(k[:, sl], cd, sd, ROPE_ROT // 2)
        k_ref[:, sl] = kj
        qk_ref[:, 512 + LANES * j:512 + LANES * (j + 1)] = kj.astype(BF16)
    v = proj(_OFF_VD, _OFF_QR)
    tm = v.shape[0]
    for j in range(DIFF_H):
        v_ref[pl.ds(j, tm, stride=DIFF_H), :] = v[:, DIFF_DV * j:DIFF_DV * (j + 1)]
    vt_ref[0] = v.T.astype(BF16)

    r = proj(_OFF_QR, _OFF_GR)
    for j in range(2):
        sl = slice(LANES * j, LANES * (j + 1))
        ret_ref[:, sl] = rope(r[:, sl], cr, sr, RET_DK // 2).astype(BF16)
    for j in range(2, 4):
        sl = slice(LANES * j, LANES * (j + 1))
        ret_ref[:, sl] = (rope(r[:, sl], cr, sr, RET_DK // 2) * (RET_DK ** -0.5)).astype(BF16)
    ret_ref[:, 512:768] = r[:, 512:768].astype(BF16)
    gr_ref[...] = proj(_OFF_GR, _D_IN)


def _inproj(x2d, g, w_bf, layer, table, tm):
    M, D = x2d.shape
    n_t = table.shape[0] // tm
    row = lambda i: (i, 0)
    return pl.pallas_call(
        _inproj_kernel,
        grid=(M // tm,),
        in_specs=[
            pl.BlockSpec((tm, D), row),
            pl.BlockSpec((1, D), lambda i: (0, 0)),
            pl.BlockSpec((None, D, _D_IN), lambda i: (layer, 0, 0)),
            pl.BlockSpec((tm, 512), lambda i: (i % n_t, 0)),
        ],
        out_specs=[
            pl.BlockSpec((tm, CONV_W), row),
            pl.BlockSpec((tm, 1024), row),
            pl.BlockSpec((tm, 512), row),
            pl.BlockSpec((tm * DIFF_H, DIFF_DV), row),
            pl.BlockSpec((1, DIFF_W, tm), lambda i: (i, 0, 0)),
            pl.BlockSpec((tm, 768), row),
            pl.BlockSpec((tm, RET_W), row),
        ],
        out_shape=[
            jax.ShapeDtypeStruct((M, CONV_W), F32),
            jax.ShapeDtypeStruct((M, 1024), BF16),
            jax.ShapeDtypeStruct((M, 512), F32),
            jax.ShapeDtypeStruct((M * DIFF_H, DIFF_DV), F32),
            jax.ShapeDtypeStruct((M // tm, DIFF_W, tm), BF16),
            jax.ShapeDtypeStruct((M, 768), BF16),
            jax.ShapeDtypeStruct((M, RET_W), F32),
        ],
        compiler_params=_params("parallel"),
    )(x2d, g.reshape(1, D), w_bf, table)


def _attn_prompt_kernel(q_ref, k_ref, vt_ref, lam_ref, sub_ref, o_ref,
                        qs, sa_ref, sb_ref, m_ref, l_ref, acc_ref, *, T, NC, nq, lam_init):
    C = T // NC
    S = nq * T
    q = q_ref[...]
    lane = lax.broadcasted_iota(jnp.int32, (S, LANES), 1)
    qs[0] = jnp.where(lane < DIFF_DQK, q, jnp.zeros_like(q))
    qs[1] = jnp.where(lane >= DIFF_DQK, q, jnp.zeros_like(q))
    lam = _lam(lam_ref, lam_init)
    chains = [(mp, c) for mp in range(2) for c in range(NC)]

    def scores(qi, ki, s_ref):
        k = k_ref[ki * T:(ki + 1) * T, :]
        for mp, c in chains:
            cols = slice(c * C, (c + 1) * C)
            s_ref[mp, :, cols] = lax.dot_general(
                k, qs[mp, qi * T + c * C:qi * T + (c + 1) * C, :], _NT, preferred_element_type=F32)

    def accumulate(qi, ki, s_ref):
        diagonal = ki == qi
        for mp, c in chains:
            cols = slice(c * C, (c + 1) * C)
            nk = (c + 1) * C if diagonal else T
            s = s_ref[mp, 0:nk, cols]
            if diagonal:
                key = lax.broadcasted_iota(jnp.int32, (nk, C), 0)
                qq = lax.broadcasted_iota(jnp.int32, (nk, C), 1) + c * C
                s = jnp.where(key <= qq, s, -jnp.inf)
            pv = lambda p: jnp.dot(vt_ref[ki, :, 0:nk], p.astype(BF16),
                                   preferred_element_type=F32)
            if ki == 0:
                m_new = jnp.max(s, axis=0, keepdims=True)
                p = jnp.exp2(s - m_new)
                l_ref[mp, :, cols] = jnp.sum(p, axis=0, keepdims=True)
                acc_ref[mp, :, cols] = pv(p)
            else:
                m_prev = m_ref[mp, :, cols]
                m_new = jnp.maximum(m_prev, jnp.max(s, axis=0, keepdims=True))
                a = jnp.exp2(m_prev - m_new)
                p = jnp.exp2(s - m_new)
                l_ref[mp, :, cols] = a * l_ref[mp, :, cols] + jnp.sum(p, axis=0, keepdims=True)
                acc_ref[mp, :, cols] = a * acc_ref[mp, :, cols] + pv(p)
            m_ref[mp, :, cols] = m_new

    def finish(qi):
        d = acc_ref[0] / l_ref[0] - lam * (acc_ref[1] / l_ref[1])
        r = lax.rsqrt(jnp.mean(d * d, axis=0, keepdims=True) + EPS)
        y = d * r * (sub_ref[...] * (1.0 - lam_init))
        o_ref[qi * T:(qi + 1) * T, :] = y.T.astype(BF16)

    pairs = [(qi, ki) for qi in range(nq) for ki in range(qi + 1)]
    bufs = (sa_ref, sb_ref)
    scores(*pairs[0], bufs[0])
    for t, (qi, ki) in enumerate(pairs):
        if t + 1 < len(pairs):
            scores(*pairs[t + 1], bufs[(t + 1) % 2])
        accumulate(qi, ki, bufs[t % 2])
        if ki == qi:
            finish(qi)


def _attn_prompt(qk, vt, lam4, subln, lam_init, B, S, T, NC=2):
    M = qk.shape[0]
    nq = S // T
    return pl.pallas_call(
        functools.partial(_attn_prompt_kernel, T=T, NC=NC, nq=nq, lam_init=lam_init),
        grid=(B, DIFF_H),
        in_specs=[
            pl.BlockSpec((S, LANES), lambda b, h: (b, h)),
            pl.BlockSpec((S, LANES), lambda b, h: (b, DIFF_H + h)),
            pl.BlockSpec((nq, DIFF_DV, T), lambda b, h: (b, h, 0)),
            pl.BlockSpec((4, DIFF_DQK), lambda b, h: (0, 0)),
            pl.BlockSpec((DIFF_DV, 1), lambda b, h: (0, 0)),
        ],
        out_specs=pl.BlockSpec((S, LANES), lambda b, h: (b, h)),
        out_shape=jax.ShapeDtypeStruct((M, DIFF_W), BF16),
        scratch_shapes=[
            pltpu.VMEM((2, S, LANES), BF16),
            pltpu.VMEM((2, T, T), F32),
            pltpu.VMEM((2, T, T), F32),
            pltpu.VMEM((2, 1, T), F32),
            pltpu.VMEM((2, 1, T), F32),
            pltpu.VMEM((2, DIFF_DV, T), F32),
        ],
        compiler_params=_params("parallel", "parallel"),
    )(qk, qk, vt, lam4, subln.reshape(DIFF_DV, 1))


def _group_norm_gate(o, g, gn):
    hid = lax.broadcasted_iota(jnp.int32, (1, RET_W), 1) // RET_DV

    def group_mean(x):
        out = jnp.zeros_like(x)
        for h in range(RET_H):
            mh = hid == h
            s = jnp.sum(jnp.where(mh, x, 0.0), axis=-1, keepdims=True) * (1.0 / RET_DV)
            out = jnp.where(mh, s, out)
        return out

    d = o - group_mean(o)
    var = group_mean(d * d)
    return d * lax.rsqrt(var + EPS) * gn * _silu(g)


def _ret_prompt_kernel(q_ref, k_ref, v_ref, g_ref, gn_ref, dec_ref, cross_ref, tail_ref, gl_ref,
                       o_ref, s_out_ref, S_ref):
    c = pl.program_id(1)

    @pl.when(c == 0)
    def _():
        S_ref[...] = jnp.zeros(S_ref.shape, F32)

    q = q_ref[...]
    k = k_ref[...]
    v = v_ref[...]
    hid = lax.broadcasted_iota(jnp.int32, (1, RET_W), 1) // RET_DV
    Sb = S_ref[...]
    o = jnp.dot(q, Sb.astype(BF16), preferred_element_type=F32) * cross_ref[...]
    for h in range(RET_H):
        mh = hid == h
        qh = jnp.where(mh, q, jnp.zeros_like(q))
        sc = lax.dot_general(qh, k, _NT, preferred_element_type=F32) * dec_ref[h]
        oh = jnp.dot(sc.astype(BF16), v, preferred_element_type=F32)
        o = o + jnp.where(mh, oh, 0.0)
    kt = (k.astype(F32) * tail_ref[...]).astype(BF16)
    upd = lax.dot_general(kt, v, (((0,), (0,)), ((), ())), preferred_element_type=F32)
    rowh = lax.broadcasted_iota(jnp.int32, (RET_W, RET_W), 0) // RET_DK
    colh = lax.broadcasted_iota(jnp.int32, (RET_W, RET_W), 1) // RET_DV
    S_new = gl_ref[...] * Sb + jnp.where(rowh == colh, upd, 0.0)
    S_ref[...] = S_new
    o_ref[...] = _group_norm_gate(o, g_ref[...], gn_ref[...]).astype(BF16)

    @pl.when(c == pl.num_programs(1) - 1)
    def _():
        y = S_new
        for sft in (64, 128, 192):
            y = y + pltpu.roll(S_new, sft, 1)
        s_out_ref[0] = y[:, 0:RET_DV]


def _ret_prompt(ret, gr, gn, B, S, L):
    M = ret.shape[0]
    nc = S // L
    decay, cross, tail, gl = _ret_tables(L)
    rep = lambda a: jnp.repeat(a, RET_DV, axis=-1)
    row = lambda b, c: (b * nc + c, 0)
    const2 = lambda b, c: (0, 0)
    o_r, s_fin = pl.pallas_call(
        _ret_prompt_kernel,
        grid=(B, nc),
        in_specs=[
            pl.BlockSpec((L, RET_W), lambda b, c: (b * nc + c, 0)),
            pl.BlockSpec((L, RET_W), lambda b, c: (b * nc + c, 1)),
            pl.BlockSpec((L, RET_W), lambda b, c: (b * nc + c, 2)),
            pl.BlockSpec((L, RET_W), row),
            pl.BlockSpec((1, RET_W), const2),
            pl.BlockSpec((RET_H, L, L), lambda b, c: (0, 0, 0)),
            pl.BlockSpec((L, RET_W), const2),
            pl.BlockSpec((L, RET_W), const2),
            pl.BlockSpec((1, RET_W), const2),
        ],
        out_specs=[
            pl.BlockSpec((L, RET_W), row),
            pl.BlockSpec((1, RET_W, RET_DV), lambda b, c: (b, 0, 0)),
        ],
        out_shape=[
            jax.ShapeDtypeStruct((M, RET_W), BF16),
            jax.ShapeDtypeStruct((B, RET_W, RET_DV), F32),
        ],
        scratch_shapes=[pltpu.VMEM((RET_W, RET_W), F32)],
        compiler_params=_params("parallel", "arbitrary"),
    )(ret, ret, ret, gr, gn.reshape(1, RET_W), decay, rep(cross), rep(tail), rep(gl[None, :]))
    return o_r, s_fin.reshape(B, RET_H, RET_DK, RET_DV)


def _conv_finish(acc, b, g, bb):
    y = acc + b
    mu = jnp.mean(y, axis=-1, keepdims=True)
    d = y - mu
    var = jnp.mean(d * d, axis=-1, keepdims=True)
    return _silu(d * lax.rsqrt(var + EPS) * g + bb)


def _conv_prompt_kernel(prev_ref, cur_ref, w_ref, b_ref, g_ref, bb_ref, o_ref, buf, sh, *, tc, sub):
    i = pl.program_id(1)
    prev = prev_ref[...]
    buf[0:CONV_HALO, :] = jnp.where(i == 0, jnp.zeros_like(prev), prev)
    buf[CONV_HALO:CONV_HALO + tc, :] = cur_ref[...]
    off = CONV_HALO - (CONV_K - 1)
    cls = [[j for j in range(CONV_K) if (off + j) % 8 == a] for a in range(8)]
    for a, taps in enumerate(cls):
        n = tc + taps[-1] - taps[0]
        sh[a, 0:n, :] = buf[off + taps[0]:off + taps[0] + n, :]

    for r0 in range(0, tc, sub):
        acc = jnp.zeros((sub, CONV_W), F32)
        for a, taps in enumerate(cls):
            for j in taps:
                acc = acc + sh[a, r0 + j - taps[0]:r0 + j - taps[0] + sub, :] * w_ref[j:j + 1, :]
        o_ref[r0:r0 + sub, :] = _conv_finish(acc, b_ref[...], g_ref[...], bb_ref[...]).astype(BF16)


def _conv_prompt(u, w, b, g, bb, B, S, tc, sub=64):
    M = u.shape[0]
    nb = S // tc
    hb = tc // CONV_HALO
    vec = lambda a: a.reshape(1, CONV_W)
    const2 = lambda bi, i: (0, 0)
    return pl.pallas_call(
        functools.partial(_conv_prompt_kernel, tc=tc, sub=sub),
        grid=(B, nb),
        in_specs=[
            pl.BlockSpec((CONV_HALO, CONV_W),
                         lambda bi, i: (jnp.maximum((bi * nb + i) * hb - 1, 0), 0)),
            pl.BlockSpec((tc, CONV_W), lambda bi, i: (bi * nb + i, 0)),
            pl.BlockSpec((CONV_HALO, CONV_W), const2),
            pl.BlockSpec((1, CONV_W), const2),
            pl.BlockSpec((1, CONV_W), const2),
            pl.BlockSpec((1, CONV_W), const2),
        ],
        out_specs=pl.BlockSpec((tc, CONV_W), lambda bi, i: (bi * nb + i, 0)),
        out_shape=jax.ShapeDtypeStruct((M, CONV_W), BF16),
        scratch_shapes=[pltpu.VMEM((CONV_HALO + tc, CONV_W), F32),
                        pltpu.VMEM((8, tc + CONV_HALO - 8, CONV_W), F32)],
        compiler_params=_params("parallel", "arbitrary"),
    )(u, u, jnp.pad(w, ((0, CONV_HALO - CONV_K), (0, 0))), vec(b), vec(g), vec(bb))


def _outproj_mlp_kernel(x_ref, c_ref, od_ref, or_ref, wo_ref, g_ref, wu_ref, wd_ref, gf_ref,
                        o_ref, a_ref, *, final_norm, tf):
    mix = jnp.dot(c_ref[...].astype(BF16), wo_ref[0:CONV_W, :], preferred_element_type=F32)
    mix += jnp.dot(od_ref[...].astype(BF16), wo_ref[CONV_W:CONV_W + DIFF_W, :],
                   preferred_element_type=F32)
    mix += jnp.dot(or_ref[...].astype(BF16), wo_ref[CONV_W + DIFF_W:, :],
                   preferred_element_type=F32)
    xm = x_ref[...] + mix
    h2 = _rms(xm, g_ref[...]).astype(BF16)
    for f0 in range(0, wu_ref.shape[1], tf):
        up = jnp.dot(h2, wu_ref[:, f0:f0 + tf], preferred_element_type=F32)
        a_ref[:, f0:f0 + tf] = jnp.square(jnp.maximum(up, 0.0)).astype(BF16)
    y = xm + jnp.dot(a_ref[...], wd_ref[...], preferred_element_type=F32)
    if final_norm:
        y = _rms(y, gf_ref[...])
    o_ref[...] = y


def _outproj_mlp(x2d, c, od, orr, wo_bf, g, wu_bf, wd_bf, gf, layer, final_norm, tm, tf):
    M, D = x2d.shape
    FF = wu_bf.shape[2]
    row = lambda i: (i, 0)
    resident = lambda shape: pl.BlockSpec(shape, lambda i: (0, 0), pipeline_mode=pl.Buffered(1))
    weight = lambda shape: pl.BlockSpec((None,) + shape, lambda i: (layer, 0, 0),
                                        pipeline_mode=pl.Buffered(1))
    return pl.pallas_call(
        functools.partial(_outproj_mlp_kernel, final_norm=final_norm, tf=tf),
        grid=(M // tm,),
        in_specs=[
            pl.BlockSpec((tm, D), row),
            pl.BlockSpec((tm, CONV_W), row),
            pl.BlockSpec((tm, DIFF_W), row),
            pl.BlockSpec((tm, RET_W), row),
            weight((D, D)),
            resident((1, D)),
            weight((D, FF)),
            weight((FF, D)),
            resident((1, D)),
        ],
        out_specs=pl.BlockSpec((tm, D), row),
        out_shape=jax.ShapeDtypeStruct((M, D), F32),
        scratch_shapes=[pltpu.VMEM((tm, FF), BF16)],
        compiler_params=_params("parallel"),
    )(x2d, c, od, orr, wo_bf, g.reshape(1, D), wu_bf, wd_bf, gf.reshape(1, D))


def _attn_sample_kernel(pt_ref, q_ref, kn_ref, vn_ref, lam_ref, sub_ref, *rest,
                        P, TD, page, lam_init):
    kp = rest[0:P]
    vp = rest[P:2 * P]
    o_ref, wq, m_ref, l_ref, acc_ref = rest[2 * P:]
    R = 8 * TD
    s_id = pl.program_id(1)

    @pl.when(s_id == 0)
    def _():
        q = q_ref[0]
        hm = lax.broadcasted_iota(jnp.int32, (8, 512), 0)
        lh = lax.broadcasted_iota(jnp.int32, (8, 512), 1) // DIFF_DQK
        for t in range(TD):
            wq[8 * t:8 * t + 8, :] = jnp.where(hm == lh, jnp.broadcast_to(q[t:t + 1, :], (8, 512)), 0.0)
        m_ref[...] = jnp.full(m_ref.shape, -jnp.inf, F32)
        l_ref[...] = jnp.zeros(l_ref.shape, F32)
        acc_ref[...] = jnp.zeros(acc_ref.shape, F32)

    def update(s, vmat):
        m_prev = m_ref[...]
        m_new = jnp.maximum(m_prev, jnp.max(s, axis=-1, keepdims=True))
        a = jnp.exp2(m_prev - m_new)
        p = jnp.exp2(s - m_new)
        l_ref[...] = a * l_ref[...] + jnp.sum(p, axis=-1, keepdims=True)
        acc_ref[...] = a * acc_ref[...] + jnp.dot(p.astype(BF16), vmat, preferred_element_type=F32)
        m_ref[...] = m_new

    wqb = wq[...].astype(BF16)
    kcat = jnp.concatenate([r[0].astype(BF16) for r in kp], axis=1)

    def v_page(r):
        heads = [r[0, pl.ds(h, page, stride=DIFF_H), :] for h in range(DIFF_H)]
        return jnp.concatenate(heads, axis=1).astype(BF16)

    vcat = jnp.concatenate([v_page(r) for r in vp], axis=0)
    update(jnp.dot(wqb, kcat, preferred_element_type=F32), vcat)

    @pl.when(s_id == pl.num_programs(1) - 1)
    def _():
        pad = jnp.zeros((16 - TD, 512), F32)
        kn = jnp.concatenate([kn_ref[0], pad], axis=0).astype(BF16)
        vn = jnp.concatenate([vn_ref[0], pad], axis=0).astype(BF16)
        s2 = lax.dot_general(wqb, kn, _NT, preferred_element_type=F32)
        tq = lax.broadcasted_iota(jnp.int32, (R, 16), 0) // 8
        tk = lax.broadcasted_iota(jnp.int32, (R, 16), 1)
        update(jnp.where(tk <= tq, s2, -jnp.inf), vn)

        lam = _lam(lam_ref, lam_init)
        o = acc_ref[...] / l_ref[...]
        hm = lax.broadcasted_iota(jnp.int32, (8, 512), 0)
        lh = lax.broadcasted_iota(jnp.int32, (8, 512), 1) // DIFF_DV
        coef = jnp.where(hm // 2 == lh, jnp.where(hm % 2 == 0, 1.0, -lam), 0.0)
        for t in range(TD):
            d = jnp.sum(o[8 * t:8 * t + 8, :] * coef, axis=0, keepdims=True)
            for h in range(DIFF_H):
                sl = slice(DIFF_DV * h, DIFF_DV * (h + 1))
                o_ref[0, t:t + 1, sl] = _rms(d[:, sl], sub_ref[...]) * (1.0 - lam_init)


def _attn_sample(q3, kn3, vn3, cache_kt3, cache_v3, pt_flat, lam4, subln, lam_init, layer, n_phys,
                 n_pages, P):
    DB, TD, _ = q3.shape
    page = cache_kt3.shape[2]
    base = layer * n_phys

    def page_map(j):
        return lambda b, s, pt: (base + pt[b * n_pages + s * P + j], 0, 0)

    tok = pl.BlockSpec((1, TD, 512), lambda b, s, pt: (b, 0, 0))
    in_specs = [
        tok, tok, tok,
        pl.BlockSpec((4, DIFF_DQK), lambda b, s, pt: (0, 0)),
        pl.BlockSpec((1, DIFF_DV), lambda b, s, pt: (0, 0)),
    ]
    in_specs += [pl.BlockSpec((1, 512, page), page_map(j)) for j in range(P)]
    in_specs += [pl.BlockSpec((1, page * DIFF_H, DIFF_DV), page_map(j)) for j in range(P)]
    R = 8 * TD
    grid_spec = pltpu.PrefetchScalarGridSpec(
        num_scalar_prefetch=1,
        grid=(DB, n_pages // P),
        in_specs=in_specs,
        out_specs=pl.BlockSpec((1, TD, 512), lambda b, s, pt: (b, 0, 0)),
        scratch_shapes=[
            pltpu.VMEM((R, 512), F32),
            pltpu.VMEM((R, 1), F32),
            pltpu.VMEM((R, 1), F32),
            pltpu.VMEM((R, 512), F32),
        ],
    )
    return pl.pallas_call(
        functools.partial(_attn_sample_kernel, P=P, TD=TD, page=page, lam_init=lam_init),
        grid_spec=grid_spec,
        out_shape=jax.ShapeDtypeStruct((DB, TD, 512), F32),
        compiler_params=_params("parallel", "arbitrary"),
    )(pt_flat, q3, kn3, vn3, lam4, subln.reshape(1, DIFF_DV),
      *([cache_kt3] * P), *([cache_v3] * P))


def _ret_sample_kernel(q_ref, k_ref, kT_ref, v_ref, g_ref, gn_ref, S_ref, dec_ref, cross_ref,
                       tail_ref, gl_ref, o_ref, s_out_ref, *, TD):
    q = q_ref[...]
    k = k_ref[...]
    v = v_ref[...]
    S = S_ref[...]
    sc = jnp.einsum('gtd,gsd->gts', q, k, preferred_element_type=F32) * dec_ref[...]
    o = jnp.einsum('gtd,gde->gte', q, S, preferred_element_type=F32) * cross_ref[...]
    for s in range(TD):
        o = o + sc[:, :, s:s + 1] * v[:, s:s + 1, :]
    kT = kT_ref[...]
    vt = v * tail_ref[...]
    S_new = gl_ref[...] * S
    for s in range(TD):
        S_new = S_new + kT[:, :, s:s + 1] * vt[:, s:s + 1, :]
    s_out_ref[...] = S_new
    mu = jnp.mean(o, axis=-1, keepdims=True)
    d = o - mu
    var = jnp.mean(d * d, axis=-1, keepdims=True)
    o_ref[...] = d * lax.rsqrt(var + EPS) * gn_ref[...] * _silu(g_ref[...])


def _ret_sample(q, k, v, g, gn, S_prev, TD, Gb):
    G = q.shape[0]
    lg = jnp.tile(_log_gamma(), G // RET_H)
    decay, cross, tail, gl = _ret_tables(TD)
    dec_g = jnp.tile(decay, (G // RET_H, 1, 1))
    cross_g = jnp.tile(cross.T, (G // RET_H, 1))[:, :, None]
    tail_g = jnp.tile(tail.T, (G // RET_H, 1))[:, :, None]
    gl_g = jnp.tile(gl, G // RET_H)[:, None, None]
    del lg
    kT = jnp.swapaxes(k, 1, 2)
    b3 = lambda shp: pl.BlockSpec((Gb,) + shp, lambda i: (i, 0, 0))
    return pl.pallas_call(
        functools.partial(_ret_sample_kernel, TD=TD),
        grid=(G // Gb,),
        in_specs=[b3((TD, 64)), b3((TD, 64)), b3((64, TD)), b3((TD, 64)), b3((TD, 64)), b3((1, 64)),
                  b3((64, 64)), b3((TD, TD)), b3((TD, 1)), b3((TD, 1)), b3((1, 1))],
        out_specs=[b3((TD, 64)), b3((64, 64))],
        out_shape=[jax.ShapeDtypeStruct((G, TD, 64), F32), jax.ShapeDtypeStruct((G, 64, 64), F32)],
        compiler_params=_params("parallel"),
    )(q, k, kT, v, g, gn, S_prev, dec_g, cross_g, tail_g, gl_g)


def _conv_sample_kernel(ext_ref, w_ref, b_ref, g_ref, bb_ref, o_ref, *, TD):
    acc = jnp.zeros(o_ref.shape, F32)
    for j in range(CONV_K):
        acc = acc + ext_ref[:, j:j + TD, :] * w_ref[j:j + 1, :]
    o_ref[...] = _conv_finish(acc, b_ref[...], g_ref[...], bb_ref[...])


def _conv_sample(ext, w, b, g, bb, TD):
    DB = ext.shape[0]
    vec = lambda a: a.reshape(1, CONV_W)
    return pl.pallas_call(
        functools.partial(_conv_sample_kernel, TD=TD),
        out_shape=jax.ShapeDtypeStruct((DB, TD, CONV_W), F32),
    )(ext, w, vec(b), vec(g), vec(bb))


def _pick(n, pref):
    t = min(n, pref)
    while n % t:
        t //= 2
    return t


def kernel(x_prompt, x_sample, cache_k, cache_v, state_conv, state_ret, page_table, norm_mix, w_in, conv_w, conv_b, conv_ln_g, conv_ln_b, lam_q1, lam_k1, lam_q2, lam_k2, diff_subln, ret_gn, w_out, norm_mlp, w_up, w_down, norm_final):
    B, S, D = x_prompt.shape
    DB, TD, _ = x_sample.shape
    depth = w_in.shape[0]
    n_phys, page = cache_k.shape[1], cache_k.shape[2]
    n_pages = page_table.shape[1]
    past_len = n_pages * page
    Mp, Ms = B * S, DB * TD

    T_attn = _pick(S, 512)
    tm_in = T_attn
    L_ret = _pick(S, 256)
    tc_conv = _pick(S, 512)
    tm_mlp = _pick(Mp, 512)
    tf_mlp = _pick(w_up.shape[2], 1024)
    P = _pick(n_pages, 16)
    Gb = _pick(DB * RET_H, 8)

    w_in_bf = w_in.astype(BF16)
    w_out_bf = w_out.astype(BF16)
    w_up_bf = w_up.astype(BF16)
    w_down_bf = w_down.astype(BF16)

    tab_p = _rope_table(jnp.arange(S))
    tab_s = jnp.tile(_rope_table(past_len + jnp.arange(TD)), (DB, 1))
    lam_all = jnp.stack([lam_q1, lam_k1, lam_q2, lam_k2], axis=1)
    cache_kt3 = jnp.transpose(cache_k, (0, 1, 3, 4, 2)).reshape(
        depth * n_phys, 2 * DIFF_H * DIFF_DQK, page)
    cache_v3 = cache_v.reshape(depth * n_phys, page * DIFF_H, DIFF_DV)
    pt_flat = page_table.reshape(-1)

    xp = x_prompt.reshape(Mp, D)
    xs = x_sample.reshape(Ms, D)
    kp, vp, cp, rp, ks_, vs_, cs_, rs_ = [], [], [], [], [], [], [], []
    for l in range(depth):
        lam_init = 0.8 - 0.6 * math.exp(-0.3 * l)
        last = l == depth - 1

        u, qk, k_d, v_d, vt, ret, g_r = _inproj(xp, norm_mix[l], w_in_bf, l, tab_p, tm_in)
        o_d = _attn_prompt(qk, vt, lam_all[l], diff_subln[l], lam_init, B, S, T_attn)
        o_r, s_fin = _ret_prompt(ret, g_r, ret_gn[l], B, S, L_ret)
        c = _conv_prompt(u, conv_w[l], conv_b[l], conv_ln_g[l], conv_ln_b[l], B, S, tc_conv)
        xp = _outproj_mlp(xp, c, o_d, o_r, w_out_bf, norm_mlp[l], w_up_bf, w_down_bf,
                          norm_final, l, last, tm_mlp, tf_mlp)
        kp.append(k_d.reshape(B, S, 2 * DIFF_H, DIFF_DQK))
        vp.append(v_d.reshape(B, S, DIFF_H, DIFF_DV))
        cp.append(u.reshape(B, S, CONV_W)[:, S - (CONV_K - 1):])
        rp.append(s_fin)

        u, qk, k_d, v_d, _, ret, g_r = _inproj(xs, norm_mix[l], w_in_bf, l, tab_s, Ms)
        q3 = qk[:, 0:512].astype(F32).reshape(DB, TD, 512)
        o_d = _attn_sample(q3, k_d.reshape(DB, TD, 512), v_d.reshape(DB, TD, 512), cache_kt3,
                           cache_v3, pt_flat, lam_all[l], diff_subln[l], lam_init,
                           l, n_phys, n_pages, P)
        to_g = lambda a: a.astype(F32).reshape(DB, TD, RET_H, 64).swapaxes(1, 2).reshape(
            DB * RET_H, TD, 64)
        gn_g = jnp.tile(ret_gn[l].reshape(RET_H, 1, RET_DV), (DB, 1, 1))
        o_r, s_new = _ret_sample(to_g(ret[:, 0:256]), to_g(ret[:, 256:512]), to_g(ret[:, 512:768]),
                                 to_g(g_r), gn_g, state_ret[l].reshape(DB * RET_H, RET_DK, RET_DV),
                                 TD, Gb)
        o_r = o_r.reshape(DB, RET_H, TD, RET_DV).swapaxes(1, 2).reshape(Ms, RET_W)
        ext = jnp.concatenate([state_conv[l], u.reshape(DB, TD, CONV_W)], axis=1)
        c = _conv_sample(ext, conv_w[l], conv_b[l], conv_ln_g[l], conv_ln_b[l], TD)
        xs = _outproj_mlp(xs, c.reshape(Ms, CONV_W), o_d.reshape(Ms, DIFF_W), o_r, w_out_bf,
                          norm_mlp[l], w_up_bf, w_down_bf, norm_final, l, last, Ms, tf_mlp)
        ks_.append(k_d.reshape(DB, TD, 2 * DIFF_H, DIFF_DQK))
        vs_.append(v_d.reshape(DB, TD, DIFF_H, DIFF_DV))
        cs_.append(ext[:, TD:])
        rs_.append(s_new.reshape(DB, RET_H, RET_DK, RET_DV))

    return (xp.reshape(B, S, D), xs.reshape(DB, TD, D),
            jnp.stack(kp), jnp.stack(vp), jnp.stack(cp), jnp.stack(rp),
            jnp.stack(ks_), jnp.stack(vs_), jnp.stack(cs_), jnp.stack(rs_))
```
